```python
import jax, jax.numpy as jnp
from jax import lax
import numpy as np

D_MODEL = 1024
BATCH = 4
SEQ = 4096
DEPTH = 1
DEC_BATCH = 128
DEC_SEQ = 8
PAST_LEN = 8192
PAGE_SIZE = 128

N_HEADS = 8
HEAD_DIM = 64
ATTN_WIDTH = N_HEADS * HEAD_DIM
CONV_GROUPS = 8
CONV_WIDTH = D_MODEL - ATTN_WIDTH
MIX_WIDTH = ATTN_WIDTH + CONV_WIDTH
IN_COLS = 3 * ATTN_WIDTH + 3 * CONV_WIDTH
ROT_DIM = HEAD_DIM // 4
ROPE_THETA = 500000.0
MOBA_BLOCK = 256
MOBA_TOPK = 3
CONV_K = 3
D_FF = 2816
Q_CHUNK = 128
ALPHA = (2.0 * DEPTH) ** 0.25
BETA = (8.0 * DEPTH) ** -0.25
LN_EPS = 1e-5

kernel_name = "hymba_shortconv_moba_convffn_deepnorm_step"


def rope_partial(x, pos):
    half = ROT_DIM // 2
    inv = 1.0 / (ROPE_THETA ** (jnp.arange(half, dtype=jnp.float32) / half))
    ang = pos.astype(jnp.float32)[:, None] * inv[None, :]
    cos = jnp.cos(ang)[:, None, :]
    sin = jnp.sin(ang)[:, None, :]
    xr = x[..., :ROT_DIM].astype(jnp.float32)
    x1, x2 = xr[..., :half], xr[..., half:]
    rot = jnp.concatenate([x1 * cos - x2 * sin, x2 * cos + x1 * sin], axis=-1).astype(x.dtype)
    return jnp.concatenate([rot, x[..., ROT_DIM:]], axis=-1)


def layer_norm(x, g, b):
    xf = x.astype(jnp.float32)
    mu = jnp.mean(xf, axis=-1, keepdims=True)
    var = jnp.mean(jnp.square(xf - mu), axis=-1, keepdims=True)
    return ((xf - mu) * lax.rsqrt(var + LN_EPS) * g.astype(jnp.float32) + b.astype(jnp.float32)).astype(x.dtype)


def causal_dwconv(u, state, w, b=None):
    t = u.shape[1]
    full = jnp.concatenate([state, u], axis=1)
    y = full[:, 0:t] * w[0]
    for j in range(1, CONV_K):
        y = y + full[:, j:j + t] * w[j]
    if b is not None:
        y = y + b
    return y, full[:, -(CONV_K - 1):]


def to_blocks(k):
    bsz, length = k.shape[0], k.shape[1]
    nb = -(-length // MOBA_BLOCK)
    k = jnp.pad(k, ((0, 0), (0, nb * MOBA_BLOCK - length), (0, 0), (0, 0)))
    kb = k.reshape(bsz, nb, MOBA_BLOCK, N_HEADS, HEAD_DIM).transpose(0, 3, 1, 2, 4)
    return kb


def moba_attend(q, q_pos, k_blocks, v_blocks, k_means):
    bsz, nq = q.shape[0], q.shape[1]
    nb = k_blocks.shape[2]
    topk = min(MOBA_TOPK, nb)
    qf = q.astype(jnp.float32)
    own = q_pos // MOBA_BLOCK
    gate = jnp.einsum('bqhd,bhnd->bqhn', qf, k_means)
    is_past = jnp.arange(nb)[None, None, None, :] < own[None, :, None, None]
    gate = jnp.where(is_past, gate, -jnp.inf)
    _, sel = lax.top_k(gate, topk)
    sel_valid = jnp.arange(topk)[None, :] < jnp.minimum(own, topk)[:, None]
    bi = jnp.arange(bsz)[:, None, None, None]
    hi = jnp.arange(N_HEADS)[None, None, :, None]
    k_sel = k_blocks[bi, hi, sel].astype(jnp.float32)
    v_sel = v_blocks[bi, hi, sel].astype(jnp.float32)
    k_own = k_blocks[:, :, own].astype(jnp.float32)
    v_own = v_blocks[:, :, own].astype(jnp.float32)
    scale = HEAD_DIM ** -0.5
    s_sel = jnp.einsum('bqhd,bqhnkd->bqhnk', qf, k_sel) * scale
    s_sel = jnp.where(sel_valid[None, :, None, :, None], s_sel, -jnp.inf)
    s_own = jnp.einsum('bqhd,bhqkd->bqhk', qf, k_own) * scale
    key_pos = own[:, None] * MOBA_BLOCK + jnp.arange(MOBA_BLOCK)[None, :]
    s_own = jnp.where((key_pos <= q_pos[:, None])[None, :, None, :], s_own, -jnp.inf)
    n_sel = topk * MOBA_BLOCK
    logits = jnp.concatenate([s_sel.reshape(bsz, nq, N_HEADS, n_sel), s_own], axis=-1)
    p = jax.nn.softmax(logits, axis=-1)
    p_sel = p[..., :n_sel].reshape(bsz, nq, N_HEADS, topk, MOBA_BLOCK)
    p_own = p[..., n_sel:]
    out = (jnp.einsum('bqhnk,bqhnkd->bqhd', p_sel, v_sel)
           + jnp.einsum('bqhk,bhqkd->bqhd', p_own, v_own))
    return out.astype(q.dtype)


def project(x, w_in, pos):
    bsz, t = x.shape[0], x.shape[1]
    z = x @ w_in
    a, c = ATTN_WIDTH, CONV_WIDTH
    q, k, v, gb, gc, h = jnp.split(z, [a, 2 * a, 3 * a, 3 * a + c, 3 * a + 2 * c], axis=-1)
    q = rope_partial(q.reshape(bsz, t, N_HEADS, HEAD_DIM), pos)
    k = rope_partial(k.reshape(bsz, t, N_HEADS, HEAD_DIM), pos)
    v = v.reshape(bsz, t, N_HEADS, HEAD_DIM)
    return q, k, v, gb, gc * h


def conv_ffn(x, w_up, cw, cb, w_down, state):
    a, g = jnp.split(x @ w_up, 2, axis=-1)
    a_c, new_state = causal_dwconv(a, state, cw, cb)
    return (jax.nn.gelu(a_c, approximate=False) * g) @ w_down, new_state


def merge_and_post(x, attn, gb, conv_y, w_o, ln1_g, ln1_b, w_up, fcw, fcb, w_down, ln2_g, ln2_b, ffn_state):
    bsz, t = x.shape[0], x.shape[1]
    mix = jnp.concatenate([attn.reshape(bsz, t, ATTN_WIDTH), gb * conv_y], axis=-1) @ w_o
    x1 = layer_norm(ALPHA * x + mix, ln1_g, ln1_b)
    f, new_ffn_state = conv_ffn(x1, w_up, fcw, fcb, w_down, ffn_state)
    return layer_norm(ALPHA * x1 + f, ln2_g, ln2_b), new_ffn_state


def setup_inputs(seed: int = 0) -> dict:
    key = jax.random.key(seed)
    ks = jax.random.split(key, 20)
    n_pages = PAST_LEN // PAGE_SIZE
    n_phys = (DEC_BATCH * n_pages * 5) // 4
    f32 = jnp.float32
    nrm = lambda k, s, sc: jax.random.normal(k, s, f32) * sc
    perm = jax.random.permutation(ks[4], n_phys)[:DEC_BATCH * n_pages]
    return {
        "x_prompt": nrm(ks[0], (BATCH, SEQ, D_MODEL), 1.0),
        "x_sample": nrm(ks[1], (DEC_BATCH, DEC_SEQ, D_MODEL), 1.0),
        "cache_k": nrm(ks[2], (DEPTH, n_phys, PAGE_SIZE, N_HEADS, HEAD_DIM), 1.0),
        "cache_v": nrm(ks[3], (DEPTH, n_phys, PAGE_SIZE, N_HEADS, HEAD_DIM), 1.0),
        "page_table": perm.reshape(DEC_BATCH, n_pages).astype(jnp.int32),
        "state_conv": nrm(ks[5], (DEPTH, DEC_BATCH, CONV_K - 1, CONV_WIDTH), 1.0),
        "state_ffn_conv": nrm(ks[6], (DEPTH, DEC_BATCH, CONV_K - 1, D_FF), 1.0),
        "w_in": nrm(ks[7], (DEPTH, D_MODEL, IN_COLS), D_MODEL ** -0.5),
        "conv_w": nrm(ks[8], (DEPTH, CONV_K, CONV_WIDTH), CONV_K ** -0.5),
        "w_o": nrm(ks[9], (DEPTH, MIX_WIDTH, D_MODEL), BETA * MIX_WIDTH ** -0.5),
        "ln1_g": 1.0 + nrm(ks[10], (DEPTH, D_MODEL), 0.02),
        "ln1_b": nrm(ks[11], (DEPTH, D_MODEL), 0.02),
        "w_up": nrm(ks[12], (DEPTH, D_MODEL, 2 * D_FF), D_MODEL ** -0.5),
        "ffn_conv_w": nrm(ks[13], (DEPTH, CONV_K, D_FF), CONV_K ** -0.5),
        "ffn_conv_b": nrm(ks[14], (DEPTH, D_FF), 0.01),
        "w_down": nrm(ks[15], (DEPTH, D_FF, D_MODEL), BETA * D_FF ** -0.5),
        "ln2_g": 1.0 + nrm(ks[16], (DEPTH, D_MODEL), 0.02),
        "ln2_b": nrm(ks[17], (DEPTH, D_MODEL), 0.02),
    }


def reference(x_prompt, x_sample, cache_k, cache_v, page_table, state_conv, state_ffn_conv,
              w_in, conv_w, w_o, ln1_g, ln1_b, w_up, ffn_conv_w, ffn_conv_b, w_down, ln2_g, ln2_b):
    n_pages = PAST_LEN // PAGE_SIZE
    n_qc = SEQ // Q_CHUNK
    pos_p = jnp.arange(SEQ, dtype=jnp.int32)
    pos_s = PAST_LEN + jnp.arange(DEC_SEQ, dtype=jnp.int32)
    xp, xs = x_prompt, x_sample
    kp_l, vp_l, ks_l, vs_l, cp_l, cs_l, fp_l, fs_l = [], [], [], [], [], [], [], []
    for l in range(DEPTH):
        q, k, v, gb, u = project(xp, w_in[l], pos_p)
        kb, vb = to_blocks(k), to_blocks(v)
        km = jnp.mean(kb.astype(jnp.float32), axis=3)
        qc = q.reshape(BATCH, n_qc, Q_CHUNK, N_HEADS, HEAD_DIM).transpose(1, 0, 2, 3, 4)
        pc = pos_p.reshape(n_qc, Q_CHUNK)
        attn = lax.map(lambda a: moba_attend(a[0], a[1], kb, vb, km), (qc, pc))
        attn = attn.transpose(1, 0, 2, 3, 4)
        conv_y, conv_new = causal_dwconv(u, jnp.zeros((BATCH, CONV_K - 1, CONV_WIDTH), u.dtype), conv_w[l])
        xp, ffn_new = merge_and_post(xp, attn, gb, conv_y, w_o[l], ln1_g[l], ln1_b[l], w_up[l],
                                     ffn_conv_w[l], ffn_conv_b[l], w_down[l], ln2_g[l], ln2_b[l],
                                     jnp.zeros((BATCH, CONV_K - 1, D_FF), xp.dtype))
        kp_l.append(k); vp_l.append(v); cp_l.append(conv_new); fp_l.append(ffn_new)
        qs, ks_, vs_, gbs, us = project(xs, w_in[l], pos_s)
        ck, cv = cache_k[l], cache_v[l]

        def per_seq(args, ck=ck, cv=cv):
            q1, k1, v1, pt = args
            k_all = jnp.concatenate([ck[pt].reshape(n_pages * PAGE_SIZE, N_HEADS, HEAD_DIM), k1], axis=0)[None]
            v_all = jnp.concatenate([cv[pt].reshape(n_pages * PAGE_SIZE, N_HEADS, HEAD_DIM), v1], axis=0)[None]
            kb1, vb1 = to_blocks(k_all), to_blocks(v_all)
            km1 = jnp.mean(kb1.astype(jnp.float32), axis=3)
            return moba_attend(q1[None], pos_s, kb1, vb1, km1)[0]

        attn_s = lax.map(per_seq, (qs, ks_, vs_, page_table))
        conv_ys, conv_new_s = causal_dwconv(us, state_conv[l].astype(us.dtype), conv_w[l])
        xs, ffn_new_s = merge_and_post(xs, attn_s, gbs, conv_ys, w_o[l], ln1_g[l], ln1_b[l], w_up[l],
                                       ffn_conv_w[l], ffn_conv_b[l], w_down[l], ln2_g[l], ln2_b[l],
                                       state_ffn_conv[l].astype(xs.dtype))
        ks_l.append(ks_); vs_l.append(vs_); cs_l.append(conv_new_s); fs_l.append(ffn_new_s)
    return (xp, xs, jnp.stack(kp_l), jnp.stack(vp_l), jnp.stack(ks_l), jnp.stack(vs_l),
            jnp.stack(cp_l), jnp.stack(cs_l), jnp.stack(fp_l), jnp.stack(fs_l))
```

```python
import functools

import jax
import jax.numpy as jnp
import numpy as np
from jax import lax
from jax.experimental import pallas as pl
from jax.experimental.pallas import tpu as pltpu

N_HEADS = 8
HEAD_DIM = 64
ATTN_WIDTH = N_HEADS * HEAD_DIM
ROT_DIM = HEAD_DIM // 4
ROPE_THETA = 500000.0
MOBA_BLOCK = 256
MOBA_TOPK = 3
CONV_K = 3
LN_EPS = 1e-5

LANES = 128
SUBLANES = 8
ROW_TILE = 512
ROW_TILE_SHORT = 128
FF_CHUNK = 256
PAGES_PER_STEP = 8
VMEM_LIMIT = 56 * 1024 * 1024
MASK_BIAS = -1e30

_NT = (((1,), (1,)), ((), ()))


def _bf16(x):
    return x.astype(jnp.bfloat16)


def _dot(a, b):
    return jnp.dot(a, b, preferred_element_type=jnp.float32)


def _dot_nt(a, b):
    return lax.dot_general(a, b, _NT, preferred_element_type=jnp.float32)


def _layer_norm(x, g, b):
    mu = jnp.mean(x, axis=-1, keepdims=True)
    xc = x - mu
    var = jnp.mean(xc * xc, axis=-1, keepdims=True)
    return xc * lax.rsqrt(var + LN_EPS) * g + b


def _resident(arr):
    return pl.BlockSpec(arr.shape, lambda *_: (0,) * arr.ndim, pipeline_mode=pl.Buffered(1))


def _causal_conv3(u3, prev3, w_ref, cols):
    sub = lax.broadcasted_iota(jnp.int32, u3.shape, 1)
    s1 = jnp.where(sub < 1, pltpu.roll(prev3, 1, 1), pltpu.roll(u3, 1, 1))
    s2 = jnp.where(sub < 2, pltpu.roll(prev3, 2, 1), pltpu.roll(u3, 2, 1))
    return s2 * w_ref[0:1, cols] + s1 * w_ref[1:2, cols] + u3 * w_ref[2:3, cols]


def _prev_groups(u3, carry):
    return jnp.concatenate([carry[None], u3[:-1]], axis=0)


def _rope_lanes(z, ra_ref, rb_ref, rc_ref):
    outs = []
    for c in range(z.shape[1] // LANES):
        zc = z[:, c * LANES:(c + 1) * LANES]
        outs.append(zc * ra_ref[...]
                    + pltpu.roll(zc, LANES - ROT_DIM // 2, 1) * rb_ref[...]
                    + pltpu.roll(zc, ROT_DIM // 2, 1) * rc_ref[...])
    return jnp.concatenate(outs, axis=1)


def _rope_rows(zt, cos_ref, sin_ref):
    half = ROT_DIM // 2
    cos, sin = cos_ref[...], sin_ref[...]
    parts = []
    for h in range(N_HEADS):
        r0 = h * HEAD_DIM
        x1, x2 = zt[r0:r0 + half], zt[r0 + half:r0 + ROT_DIM]
        parts += [x1 * cos - x2 * sin, x2 * cos + x1 * sin, zt[r0 + ROT_DIM:r0 + HEAD_DIM]]
    return jnp.concatenate(parts, axis=0)


def _gated_conv(xb, w_ref, cw_ref, prev3_fn, cg_ref):
    a = ATTN_WIDTH
    c = cw_ref.shape[1]
    rows = xb.shape[0]
    gb = _dot(xb, w_ref[:, 3 * a:3 * a + c])
    u = _dot(xb, w_ref[:, 3 * a + c:3 * a + 2 * c]) * _dot(xb, w_ref[:, 3 * a + 2 * c:3 * a + 3 * c])
    u3 = u.reshape(rows // SUBLANES, SUBLANES, c)
    y = _causal_conv3(u3, prev3_fn(u3), cw_ref, slice(None)).reshape(rows, c)
    cg_ref[...] = _bf16(gb * y)
    return u3


def _proj_long_kernel(x_ref, w_ref, wkv_ref, ra_ref, rb_ref, rc_ref, cos_ref, sin_ref, cw_ref, st_ref,
                      q_ref, kt_ref, vt_ref, ktb_ref, vtb_ref, cg_ref, tail_ref, carry_ref):
    a = ATTN_WIDTH
    xb = _bf16(x_ref[...])
    q = _rope_lanes(_dot(xb, w_ref[:, 0:a]), ra_ref, rb_ref, rc_ref)
    q_ref[...] = _bf16(q * (HEAD_DIM ** -0.5))

    kvt = _dot_nt(wkv_ref[...], xb)
    kt = _rope_rows(kvt[0:a], cos_ref, sin_ref)
    vt = kvt[a:2 * a]
    kt_ref[0] = kt
    vt_ref[0] = vt
    for c in range(ktb_ref.shape[1]):
        ktb_ref[0, c] = _bf16(kt[:, c * MOBA_BLOCK:(c + 1) * MOBA_BLOCK])
        vtb_ref[0, c] = _bf16(vt[:, c * MOBA_BLOCK:(c + 1) * MOBA_BLOCK])

    @pl.when(pl.program_id(1) == 0)
    def _():
        carry_ref[...] = st_ref[0]

    u3 = _gated_conv(xb, w_ref, cw_ref, lambda u3: _prev_groups(u3, carry_ref[...]), cg_ref)
    carry_ref[...] = u3[-1]
    tail_ref[0] = u3[-1]


def _proj_short_kernel(x_ref, w_ref, ra_ref, rb_ref, rc_ref, cw_ref, st_ref,
                       q_ref, k_ref, v_ref, cg_ref, tail_ref):
    a = ATTN_WIDTH
    xb = _bf16(x_ref[...])
    q = _rope_lanes(_dot(xb, w_ref[:, 0:a]), ra_ref, rb_ref, rc_ref)
    q_ref[...] = _bf16(q * (HEAD_DIM ** -0.5))
    k_ref[...] = _rope_lanes(_dot(xb, w_ref[:, a:2 * a]), ra_ref, rb_ref, rc_ref)
    v_ref[...] = _dot(xb, w_ref[:, 2 * a:3 * a])
    tail_ref[...] = _gated_conv(xb, w_ref, cw_ref, lambda u3: st_ref[...], cg_ref)


def _proj_long(x2d, w_in_b, w_kvt_b, lane_tabs, row_tabs, conv_w, state8, bsz, seq):
    rows, d = x2d.shape
    tm = ROW_TILE
    nt = seq // tm
    a, c = ATTN_WIDTH, conv_w.shape[1]
    nblk = tm // MOBA_BLOCK
    row_blk = lambda b, t: (b * nt + t, 0)
    tab_spec = pl.BlockSpec((tm, LANES), lambda b, t: (t, 0))
    rtab_spec = pl.BlockSpec((ROT_DIM // 2, tm), lambda b, t: (0, t))
    st_spec = pl.BlockSpec((1, SUBLANES, c), lambda b, t: (b, 0, 0))
    kt_spec = pl.BlockSpec((1, a, tm), lambda b, t: (b, 0, t))
    ktb_spec = pl.BlockSpec((1, nblk, a, MOBA_BLOCK), lambda b, t: (b, t, 0, 0))
    f32, bf = jnp.float32, jnp.bfloat16
    out_shape = (
        jax.ShapeDtypeStruct((rows, a), bf),
        jax.ShapeDtypeStruct((bsz, a, seq), f32), jax.ShapeDtypeStruct((bsz, a, seq), f32),
        jax.ShapeDtypeStruct((bsz, seq // MOBA_BLOCK, a, MOBA_BLOCK), bf),
        jax.ShapeDtypeStruct((bsz, seq // MOBA_BLOCK, a, MOBA_BLOCK), bf),
        jax.ShapeDtypeStruct((rows, c), bf),
        jax.ShapeDtypeStruct(state8.shape, f32),
    )
    return pl.pallas_call(
        _proj_long_kernel,
        grid=(bsz, nt),
        in_specs=[pl.BlockSpec((tm, d), row_blk), _resident(w_in_b), _resident(w_kvt_b),
                  tab_spec, tab_spec, tab_spec, rtab_spec, rtab_spec, _resident(conv_w), st_spec],
        out_specs=[pl.BlockSpec((tm, a), row_blk), kt_spec, kt_spec, ktb_spec, ktb_spec,
                   pl.BlockSpec((tm, c), row_blk), st_spec],
        out_shape=out_shape,
        scratch_shapes=[pltpu.VMEM((SUBLANES, c), f32)],
        compiler_params=pltpu.CompilerParams(
            dimension_semantics=("arbitrary", "arbitrary"), vmem_limit_bytes=VMEM_LIMIT),
        name="proj_long",
    )(x2d, w_in_b, w_kvt_b, *lane_tabs, *row_tabs, conv_w, state8)


def _proj_short(x2d, w_in_b, lane_tabs, conv_w, state8):
    rows, d = x2d.shape
    tm = ROW_TILE_SHORT
    a, c = ATTN_WIDTH, conv_w.shape[1]
    row_blk = lambda t: (t, 0)
    tab_spec = pl.BlockSpec((tm, LANES), lambda t: (0, 0))
    st_spec = pl.BlockSpec((tm // SUBLANES, SUBLANES, c), lambda t: (t, 0, 0))
    f32, bf = jnp.float32, jnp.bfloat16
    out_shape = (
        jax.ShapeDtypeStruct((rows, a), bf), jax.ShapeDtypeStruct((rows, a), f32),
        jax.ShapeDtypeStruct((rows, a), f32), jax.ShapeDtypeStruct((rows, c), bf),
        jax.ShapeDtypeStruct(state8.shape, f32),
    )
    return pl.pallas_call(
        _proj_short_kernel,
        grid=(rows // tm,),
        in_specs=[pl.BlockSpec((tm, d), row_blk), _resident(w_in_b),
                  tab_spec, tab_spec, tab_spec, _resident(conv_w), st_spec],
        out_specs=[pl.BlockSpec((tm, a), row_blk)] * 3 + [pl.BlockSpec((tm, c), row_blk), st_spec],
        out_shape=out_shape,
        compiler_params=pltpu.CompilerParams(
            dimension_semantics=("arbitrary",), vmem_limit_bytes=VMEM_LIMIT),
        name="proj_short",
    )(x2d, w_in_b, *lane_tabs, conv_w, state8)


def _rope_angles(pos):
    half = ROT_DIM // 2
    inv = 1.0 / (ROPE_THETA ** (jnp.arange(half, dtype=jnp.float32) / half))
    ang = pos.astype(jnp.float32)[:, None] * inv[None, :]
    return jnp.cos(ang), jnp.sin(ang)


def _rope_lane_tables(cos, sin):
    t, half = cos.shape
    rest = HEAD_DIM - ROT_DIM
    ones, zeros = jnp.ones((t, rest), jnp.float32), jnp.zeros((t, rest), jnp.float32)
    zh = jnp.zeros((t, half), jnp.float32)
    ta = jnp.concatenate([cos, cos, ones], axis=1)
    tb = jnp.concatenate([-sin, zh, zeros], axis=1)
    tc = jnp.concatenate([zh, sin, zeros], axis=1)
    return tuple(jnp.tile(x, (1, LANES // HEAD_DIM)) for x in (ta, tb, tc))


def _top_mask(gate, n_valid, axis):
    idx = lax.broadcasted_iota(jnp.int32, gate.shape, axis)
    n = gate.shape[axis]
    g = jnp.where(idx < n_valid, gate, -jnp.inf)
    sel = jnp.zeros(gate.shape, jnp.bool_)
    for _ in range(MOBA_TOPK):
        mx = jnp.max(g, axis=axis, keepdims=True)
        first = jnp.min(jnp.where(g == mx, idx, n), axis=axis, keepdims=True)
        pick = (idx == first) & (mx > -jnp.inf)
        sel = sel | pick
        g = jnp.where(pick, -jnp.inf, g)
    return sel


def _moba_prompt_kernel(q_ref, kt_ref, vt_ref, o_ref, kmt_ref, m_ref, acc_ref):
    i = pl.program_id(2)
    blk = MOBA_BLOCK
    nb = kt_ref.shape[1]
    f32, bf = jnp.float32, jnp.bfloat16

    @pl.when(i == 0)
    def _():
        lane_e = lax.broadcasted_iota(jnp.int32, (blk, LANES), 1)
        kmt = jnp.zeros((LANES, LANES), f32)
        for n in range(nb):
            kmt = kmt + _dot(kt_ref[0, n], _bf16(jnp.where(lane_e == n, 1.0 / blk, 0.0)))
        kmt_ref[...] = kmt

    lane = lax.broadcasted_iota(jnp.int32, (blk, LANES), 1)
    head0 = _bf16(jnp.where(lane < HEAD_DIM, 1.0, 0.0))
    head1 = _bf16(jnp.where(lane < HEAD_DIM, 0.0, 1.0))
    qb = q_ref[0]
    q2 = jnp.concatenate([qb * head0, qb * head1], axis=0)
    gate = _dot(q2.astype(f32), kmt_ref[...])
    col_g = lax.broadcasted_iota(jnp.int32, gate.shape, 1)
    visible = _top_mask(gate, i, 1) | (col_g == i)
    q_aug = jnp.concatenate([q2, _bf16(jnp.where(visible, 0.0, MASK_BIAS))], axis=1)

    row_k = lax.broadcasted_iota(jnp.int32, (LANES, blk), 0)
    ones_t = jnp.ones((HEAD_DIM, blk), bf)

    def scores_and_values(j):
        k_aug = jnp.concatenate([kt_ref[0, j], _bf16(jnp.where(row_k == j, 1.0, 0.0))], axis=0)
        s = _dot(q_aug, k_aug)
        vt = vt_ref[0, j]
        v1 = jnp.concatenate([vt[:HEAD_DIM], ones_t], axis=0)
        v2 = jnp.concatenate([ones_t, vt[HEAD_DIM:]], axis=0)
        return s, v1, v2

    def accumulate(s, v1, v2, first):
        m_blk = jnp.max(s, axis=-1, keepdims=True)
        m_new = m_blk if first else jnp.maximum(m_ref[...], m_blk)
        p = _bf16(jnp.exp(s - m_new))
        pv = jnp.concatenate([_dot_nt(p[:blk], v1), _dot_nt(p[blk:], v2)], axis=0)
        if first:
            acc_ref[...] = pv
        else:
            acc_ref[...] = acc_ref[...] * jnp.exp(m_ref[...] - m_new) + pv
        m_ref[...] = m_new

    s, v1, v2 = scores_and_values(i)
    row = lax.broadcasted_iota(jnp.int32, s.shape, 0)
    col = lax.broadcasted_iota(jnp.int32, s.shape, 1)
    qpos = jnp.where(row >= blk, row - blk, row)
    accumulate(jnp.where(col <= qpos, s, -jnp.inf), v1, v2, True)

    def body(j, carry):
        s, v1, v2 = scores_and_values(j)
        accumulate(s, v1, v2, False)
        return carry

    lax.fori_loop(0, i, body, 0)

    acc = acc_ref[...]
    a1, a2 = acc[:blk], acc[blk:]
    o1 = a1 / pltpu.roll(a1, HEAD_DIM, 1)
    o2 = a2 / pltpu.roll(a2, HEAD_DIM, 1)
    o_ref[0] = _bf16(jnp.where(lane < HEAD_DIM, o1, o2))


def _moba_prompt(qb, ktb, vtb, bsz, seq):
    blk = MOBA_BLOCK
    nb = seq // blk
    q3 = qb.reshape(bsz, seq, ATTN_WIDTH)
    kv_spec = pl.BlockSpec((1, nb, LANES, blk), lambda b, h, i: (b, 0, h, 0))
    q_spec = pl.BlockSpec((1, blk, LANES), lambda b, h, i: (b, i, h))
    out = pl.pallas_call(
        _moba_prompt_kernel,
        grid=(bsz, ATTN_WIDTH // LANES, nb),
        in_specs=[q_spec, kv_spec, kv_spec],
        out_specs=q_spec,
        out_shape=jax.ShapeDtypeStruct((bsz, seq, ATTN_WIDTH), jnp.bfloat16),
        scratch_shapes=[pltpu.VMEM((LANES, LANES), jnp.float32),
                        pltpu.VMEM((2 * blk, 1), jnp.float32),
                        pltpu.VMEM((2 * blk, LANES), jnp.float32)],
        compiler_params=pltpu.CompilerParams(
            dimension_semantics=("arbitrary", "arbitrary", "arbitrary"), vmem_limit_bytes=VMEM_LIMIT),
        name="moba_prompt",
    )(q3, ktb, vtb)
    return out.reshape(bsz * seq, ATTN_WIDTH)


def _moba_sample_kernel(pt_ref, q_ref, k1_ref, v1_ref, *rest):
    del pt_ref
    n_pg = PAGES_PER_STEP
    k_pages, v_pages = rest[:n_pg], rest[n_pg:2 * n_pg]
    o_ref, wq_ref, m_ref, l_ref, g_ref, acc_ref = rest[2 * n_pg:]
    g = pl.program_id(1)
    f32 = jnp.float32
    t_new = q_ref.shape[1]
    rows = N_HEADS * t_new
    page = k_pages[0].shape[1]
    pages_per_blk = MOBA_BLOCK // page
    blks_per_step = n_pg // pages_per_blk
    row = lax.broadcasted_iota(jnp.int32, (rows, ATTN_WIDTH), 0)
    lane = lax.broadcasted_iota(jnp.int32, (rows, ATTN_WIDTH), 1)
    own_head = (row // t_new) == (lane // HEAD_DIM)

    @pl.when(g == 0)
    def _():
        q8 = q_ref[0].astype(f32)
        wq_ref[...] = _bf16(jnp.where(own_head, jnp.concatenate([q8] * N_HEADS, axis=0), 0.0))

    wq = wq_ref[...]
    for b in range(blks_per_step):
        pg = range(b * pages_per_blk, (b + 1) * pages_per_blk)
        s = jnp.concatenate([_dot(wq, _bf16(k_pages[p][...])) for p in pg], axis=1)
        m = jnp.max(s, axis=-1, keepdims=True)
        p_blk = jnp.exp(s - m)
        n = g * blks_per_step + b
        m_ref[n] = m
        l_ref[n] = jnp.sum(p_blk, axis=-1, keepdims=True)
        g_ref[n] = jnp.sum(s, axis=-1, keepdims=True)
        pb = _bf16(p_blk)
        o = _dot_nt(pb[:, 0:page], _bf16(v_pages[pg[0]][...]))
        for x, p in enumerate(pg[1:], 1):
            o = o + _dot_nt(pb[:, x * page:(x + 1) * page], _bf16(v_pages[p][...]))
        acc_ref[n] = o

    @pl.when(g == pl.num_programs(1) - 1)
    def _():
        nblk = m_ref.shape[0]
        sel = _top_mask(g_ref[...], nblk, 0)
        m_all = m_ref[...]
        zpad = jnp.zeros((LANES - t_new, ATTN_WIDTH), jnp.bfloat16)
        k_own = jnp.concatenate([_bf16(k1_ref[0]), zpad], axis=0)
        v_own = jnp.concatenate([_bf16(v1_ref[0]), zpad], axis=0)
        s_own = _dot_nt(wq, k_own)
        r_o = lax.broadcasted_iota(jnp.int32, s_own.shape, 0)
        c_o = lax.broadcasted_iota(jnp.int32, s_own.shape, 1)
        s_own = jnp.where(c_o <= r_o % t_new, s_own, -jnp.inf)
        m_fin = jnp.maximum(jnp.max(jnp.where(sel, m_all, -jnp.inf), axis=0),
                            jnp.max(s_own, axis=-1, keepdims=True))
        w = jnp.exp(jnp.where(sel, m_all - m_fin[None], -jnp.inf))
        p_own = jnp.exp(s_own - m_fin)
        den = jnp.sum(w * l_ref[...], axis=0) + jnp.sum(p_own, axis=-1, keepdims=True)
        num = jnp.sum(w * acc_ref[...], axis=0) + _dot(_bf16(p_own), v_own)
        o64 = jnp.where(own_head, num / den, 0.0)
        o8 = o64[0:t_new]
        for h in range(1, N_HEADS):
            o8 = o8 + o64[h * t_new:(h + 1) * t_new]
        o_ref[0] = _bf16(o8)


def _moba_sample(qb, k_new, v_new, cache_k, cache_v, page_table, n_seq, t_new):
    n_phys, page = cache_k.shape[0], cache_k.shape[1]
    ck = cache_k.transpose(0, 2, 3, 1).reshape(n_phys, ATTN_WIDTH, page)
    cv = cache_v.transpose(0, 2, 3, 1).reshape(n_phys, ATTN_WIDTH, page)
    n_pages = page_table.shape[1]
    n_blk = n_pages * page // MOBA_BLOCK
    rows = N_HEADS * t_new
    seq_spec = pl.BlockSpec((1, t_new, ATTN_WIDTH), lambda s, g, pt: (s, 0, 0))

    def page_spec(p):
        return pl.BlockSpec((None, ATTN_WIDTH, page), lambda s, g, pt: (pt[s, g * PAGES_PER_STEP + p], 0, 0))

    pages = [page_spec(p) for p in range(PAGES_PER_STEP)]
    f32 = jnp.float32
    out = pl.pallas_call(
        _moba_sample_kernel,
        grid_spec=pltpu.PrefetchScalarGridSpec(
            num_scalar_prefetch=1,
            grid=(n_seq, n_pages // PAGES_PER_STEP),
            in_specs=[seq_spec, seq_spec, seq_spec] + pages + pages,
            out_specs=seq_spec,
            scratch_shapes=[pltpu.VMEM((rows, ATTN_WIDTH), jnp.bfloat16),
                            pltpu.VMEM((n_blk, rows, 1), f32),
                            pltpu.VMEM((n_blk, rows, 1), f32),
                            pltpu.VMEM((n_blk, rows, 1), f32),
                            pltpu.VMEM((n_blk, rows, ATTN_WIDTH), f32)]),
        out_shape=jax.ShapeDtypeStruct((n_seq, t_new, ATTN_WIDTH), jnp.bfloat16),
        compiler_params=pltpu.CompilerParams(
            dimension_semantics=("arbitrary", "arbitrary"), vmem_limit_bytes=VMEM_LIMIT),
        name="moba_sample",
    )(page_table, qb.reshape(n_seq, t_new, ATTN_WIDTH), k_new.reshape(n_seq, t_new, ATTN_WIDTH),
      v_new.reshape(n_seq, t_new, ATTN_WIDTH), *([ck] * PAGES_PER_STEP), *([cv] * PAGES_PER_STEP))
    return out.reshape(n_seq * t_new, ATTN_WIDTH)


def _post_kernel(x_ref, at_ref, cg_ref, st_ref, wo_ref, g1_ref, b1_ref, wup_ref, fw_ref, fb_ref, wdn_ref,
                 g2_ref, b2_ref, y_ref, tail_ref, carry_ref, *, long_seq, alpha):
    a = at_ref.shape[1]
    rows = x_ref.shape[0]
    d_ff = wdn_ref.shape[0]
    mix = _dot(at_ref[...], wo_ref[0:a, :]) + _dot(cg_ref[...], wo_ref[a:, :])
    x1 = _layer_norm(alpha * x_ref[...] + mix, g1_ref[...], b1_ref[...])
    x1b = _bf16(x1)

    if long_seq:
        @pl.when(pl.program_id(1) == 0)
        def _():
            carry_ref[...] = st_ref[0]

    f = jnp.zeros(x1.shape, jnp.float32)
    for c0 in range(0, d_ff, FF_CHUNK):
        cols = slice(c0, c0 + FF_CHUNK)
        up_a = _dot(x1b, wup_ref[:, cols])
        up_g = _dot(x1b, wup_ref[:, d_ff + c0:d_ff + c0 + FF_CHUNK])
        a3 = up_a.reshape(rows // SUBLANES, SUBLANES, FF_CHUNK)
        if long_seq:
            prev3 = _prev_groups(a3, carry_ref[:, cols])
            carry_ref[:, cols] = a3[-1]
            tail_ref[0, :, cols] = a3[-1]
        else:
            prev3 = st_ref[:, :, cols]
            tail_ref[:, :, cols] = a3
        a_c = (_causal_conv3(a3, prev3, fw_ref, cols) + fb_ref[0:1, cols]).reshape(rows, FF_CHUNK)
        act = 0.5 * a_c * (1.0 + lax.erf(a_c * np.float32(np.sqrt(0.5))))
        f = f + _dot(_bf16(act * up_g), wdn_ref[cols, :])
    y_ref[...] = _layer_norm(alpha * x1 + f, g2_ref[...], b2_ref[...])


def _post(x2d, attn, cg, state8, w_o_b, ln1_g, ln1_b, w_up_b, fcw, fcb, w_dn_b, ln2_g, ln2_b,
          *, nb, nt, long_seq, alpha):
    rows, d = x2d.shape
    tm = rows // (nb * nt)
    d_ff = w_dn_b.shape[0]
    row_blk = lambda b, t: (b * nt + t, 0)
    if long_seq:
        st_spec = pl.BlockSpec((1, SUBLANES, d_ff), lambda b, t: (b, 0, 0))
    else:
        st_spec = pl.BlockSpec((tm // SUBLANES, SUBLANES, d_ff), lambda b, t: (t, 0, 0))
    vec = lambda v: v.reshape(1, -1)
    ln1_g, ln1_b, fcb, ln2_g, ln2_b = (vec(v) for v in (ln1_g, ln1_b, fcb, ln2_g, ln2_b))
    consts = (w_o_b, ln1_g, ln1_b, w_up_b, fcw, fcb, w_dn_b, ln2_g, ln2_b)
    return pl.pallas_call(
        functools.partial(_post_kernel, long_seq=long_seq, alpha=alpha),
        grid=(nb, nt),
        in_specs=[pl.BlockSpec((tm, d), row_blk),
                  pl.BlockSpec((tm, attn.shape[1]), row_blk),
                  pl.BlockSpec((tm, cg.shape[1]), row_blk),
                  st_spec] + [_resident(w) for w in consts],
        out_specs=[pl.BlockSpec((tm, d), row_blk), st_spec],
        out_shape=(jax.ShapeDtypeStruct((rows, d), jnp.float32),
                   jax.ShapeDtypeStruct(state8.shape, jnp.float32)),
        scratch_shapes=[pltpu.VMEM((SUBLANES, d_ff), jnp.float32)],
        compiler_params=pltpu.CompilerParams(
            dimension_semantics=("arbitrary", "arbitrary"), vmem_limit_bytes=VMEM_LIMIT),
        name="post_long" if long_seq else "post_short",
    )(x2d, attn, cg, state8, *consts)


def _pad_state(state):
    return jnp.pad(state, ((0, 0), (SUBLANES - (CONV_K - 1), 0), (0, 0)))


def kernel(x_prompt, x_sample, cache_k, cache_v, page_table, state_conv, state_ffn_conv, w_in, conv_w, w_o,
           ln1_g, ln1_b, w_up, ffn_conv_w, ffn_conv_b, w_down, ln2_g, ln2_b):
    depth = w_in.shape[0]
    bsz, seq, d_model = x_prompt.shape
    n_seq, t_new, _ = x_sample.shape
    conv_width = conv_w.shape[2]
    d_ff = w_down.shape[1]
    past_len = page_table.shape[1] * cache_k.shape[2]
    alpha = (2.0 * depth) ** 0.25
    a = ATTN_WIDTH
    assert t_new == SUBLANES and seq % ROW_TILE == 0 and (n_seq * t_new) % ROW_TILE_SHORT == 0
    assert past_len % MOBA_BLOCK == 0 and past_len // MOBA_BLOCK >= MOBA_TOPK and d_ff % FF_CHUNK == 0

    cos_p, sin_p = _rope_angles(jnp.arange(seq, dtype=jnp.int32))
    cos_s, sin_s = _rope_angles(past_len + jnp.arange(t_new, dtype=jnp.int32))
    lane_tabs_p = _rope_lane_tables(cos_p, sin_p)
    row_tabs_p = (cos_p.T, sin_p.T)
    lane_tabs_s = tuple(jnp.tile(t, (ROW_TILE_SHORT // t_new, 1)) for t in _rope_lane_tables(cos_s, sin_s))
    grid_p = dict(nb=bsz, nt=seq // ROW_TILE, long_seq=True)
    grid_s = dict(nb=1, nt=n_seq * t_new // ROW_TILE_SHORT, long_seq=False)
    zeros_conv = jnp.zeros((bsz, SUBLANES, conv_width), jnp.float32)
    zeros_ffn = jnp.zeros((bsz, SUBLANES, d_ff), jnp.float32)

    xp = x_prompt.reshape(bsz * seq, d_model)
    xs = x_sample.reshape(n_seq * t_new, d_model)
    outs = [[] for _ in range(8)]
    for l in range(depth):
        w_in_b, w_o_b, w_up_b, w_dn_b = _bf16(w_in[l]), _bf16(w_o[l]), _bf16(w_up[l]), _bf16(w_down[l])
        w_kvt_b = w_in_b[:, a:3 * a].T
        post_w = (w_o_b, ln1_g[l], ln1_b[l], w_up_b, ffn_conv_w[l], ffn_conv_b[l], w_dn_b, ln2_g[l], ln2_b[l])

        qb, kt, vt, ktb, vtb, cg, tail = _proj_long(
            xp, w_in_b, w_kvt_b, lane_tabs_p, row_tabs_p, conv_w[l], zeros_conv, bsz, seq)
        attn = _moba_prompt(qb, ktb, vtb, bsz, seq)
        xp, ffn_tail = _post(xp, attn, cg, zeros_ffn, *post_w, alpha=alpha, **grid_p)
        new = (kt.reshape(bsz, N_HEADS, HEAD_DIM, seq).transpose(0, 3, 1, 2),
               vt.reshape(bsz, N_HEADS, HEAD_DIM, seq).transpose(0, 3, 1, 2),
               tail[:, -(CONV_K - 1):], ffn_tail[:, -(CONV_K - 1):])
        for dst, val in zip(outs[0::2], new):
            dst.append(val)

        qb, k, v, cg, tail = _proj_short(xs, w_in_b, lane_tabs_s, conv_w[l], _pad_state(state_conv[l]))
        attn = _moba_sample(qb, k, v, cache_k[l], cache_v[l], page_table, n_seq, t_new)
        xs, ffn_tail = _post(xs, attn, cg, _pad_state(state_ffn_conv[l]), *post_w, alpha=alpha, **grid_s)
        new = (k.reshape(n_seq, t_new, N_HEADS, HEAD_DIM), v.reshape(n_seq, t_new, N_HEADS, HEAD_DIM),
               tail[:, -(CONV_K - 1):], ffn_tail[:, -(CONV_K - 1):])
        for dst, val in zip(outs[1::2], new):
            dst.append(val)

    kp, ks, vp, vs, cp, cs, fp, fs = (jnp.stack(o) for o in outs)
    return (xp.reshape(bsz, seq, d_model), xs.reshape(n_seq, t_new, d_model), kp, vp, ks, vs, cp, cs, fp, fs)
```

```python
import functools

import jax
import jax.numpy as jnp
import numpy as np
from jax import lax
from jax.experimental import pallas as pl
from jax.experimental.pallas import tpu as pltpu

N_HEADS = 8
HEAD_DIM = 64
ATTN_WIDTH = N_HEADS * HEAD_DIM
ROT_DIM = HEAD_DIM // 4
ROPE_THETA = 500000.0
MOBA_BLOCK = 256
MOBA_TOPK = 3
CONV_K = 3
LN_EPS = 1e-5

LANES = 128
SUBLANES = 8
BF16_ROWS = 16
ROW_TILE = 512
ROW_TILE_SHORT = 128
FF_CHUNK = 256
PAGES_PER_STEP = 16
VMEM_LIMIT = 56 * 1024 * 1024
MASK_BIAS = -1e30

_NT = (((1,), (1,)), ((), ()))


def _bf16(x):
    return x.astype(jnp.bfloat16)


def _dot(a, b):
    return jnp.dot(a, b, preferred_element_type=jnp.float32)


def _dot_nt(a, b):
    return lax.dot_general(a, b, _NT, preferred_element_type=jnp.float32)


def _layer_norm(x, g, b):
    mu = jnp.mean(x, axis=-1, keepdims=True)
    xc = x - mu
    var = jnp.mean(xc * xc, axis=-1, keepdims=True)
    return xc * lax.rsqrt(var + LN_EPS) * g + b


def _resident(arr):
    return pl.BlockSpec(arr.shape, lambda *_: (0,) * arr.ndim, pipeline_mode=pl.Buffered(1))


def _causal_conv3(u3, prev3, w_ref, cols):
    sub = lax.broadcasted_iota(jnp.int32, u3.shape, 1)
    s1 = jnp.where(sub < 1, pltpu.roll(prev3, 1, 1), pltpu.roll(u3, 1, 1))
    s2 = jnp.where(sub < 2, pltpu.roll(prev3, 2, 1), pltpu.roll(u3, 2, 1))
    return s2 * w_ref[0:1, cols] + s1 * w_ref[1:2, cols] + u3 * w_ref[2:3, cols]


def _prev_groups(u3, carry):
    return jnp.concatenate([carry[None], u3[:-1]], axis=0)


def _rope_lanes(z, ra_ref, rb_ref, rc_ref):
    outs = []
    for c in range(z.shape[1] // LANES):
        zc = z[:, c * LANES:(c + 1) * LANES]
        outs.append(zc * ra_ref[...]
                    + pltpu.roll(zc, LANES - ROT_DIM // 2, 1) * rb_ref[...]
                    + pltpu.roll(zc, ROT_DIM // 2, 1) * rc_ref[...])
    return jnp.concatenate(outs, axis=1)


def _rope_rows(zt, cos_ref, sin_ref):
    half = ROT_DIM // 2
    cos, sin = cos_ref[...], sin_ref[...]
    parts = []
    for h in range(N_HEADS):
        r0 = h * HEAD_DIM
        x1, x2 = zt[r0:r0 + half], zt[r0 + half:r0 + ROT_DIM]
        parts += [x1 * cos - x2 * sin, x2 * cos + x1 * sin, zt[r0 + ROT_DIM:r0 + HEAD_DIM]]
    return jnp.concatenate(parts, axis=0)


def _gated_conv(xb, w_ref, cw_ref, prev3_fn, cg_ref):
    a = ATTN_WIDTH
    c = cw_ref.shape[1]
    rows = xb.shape[0]
    gb = _dot(xb, w_ref[:, 3 * a:3 * a + c])
    u = _dot(xb, w_ref[:, 3 * a + c:3 * a + 2 * c]) * _dot(xb, w_ref[:, 3 * a + 2 * c:3 * a + 3 * c])
    u3 = u.reshape(rows // SUBLANES, SUBLANES, c)
    y = _causal_conv3(u3, prev3_fn(u3), cw_ref, slice(None)).reshape(rows, c)
    cg_ref[...] = _bf16(gb * y)
    return u3


def _proj_long_kernel(x_ref, w_ref, wt_ref, ra_ref, rb_ref, rc_ref, cos_ref, sin_ref, cw_ref, st_ref,
                      qt_ref, k_ref, kt_ref, vt_ref, vtb_ref, cg_ref, tail_ref, carry_ref):
    a = ATTN_WIDTH
    xb = _bf16(x_ref[...])
    qkvt = _dot_nt(wt_ref[...], xb)
    qt_ref[0] = _bf16(_rope_rows(qkvt[0:a], cos_ref, sin_ref) * (HEAD_DIM ** -0.5))
    kt_ref[0] = _rope_rows(qkvt[a:2 * a], cos_ref, sin_ref)
    vt = qkvt[2 * a:3 * a]
    vt_ref[0] = vt
    for c in range(vtb_ref.shape[1]):
        vtb_ref[0, c] = _bf16(vt[:, c * MOBA_BLOCK:(c + 1) * MOBA_BLOCK])
    k_ref[...] = _bf16(_rope_lanes(_dot(xb, w_ref[:, a:2 * a]), ra_ref, rb_ref, rc_ref))

    @pl.when(pl.program_id(1) == 0)
    def _():
        carry_ref[...] = st_ref[0]

    u3 = _gated_conv(xb, w_ref, cw_ref, lambda u3: _prev_groups(u3, carry_ref[...]), cg_ref)
    carry_ref[...] = u3[-1]
    tail_ref[0] = u3[-1]


def _proj_short_kernel(x_ref, w_ref, ra_ref, rb_ref, rc_ref, cw_ref, st_ref,
                       q_ref, k_ref, v_ref, cg_ref, tail_ref):
    a = ATTN_WIDTH
    xb = _bf16(x_ref[...])
    q = _rope_lanes(_dot(xb, w_ref[:, 0:a]), ra_ref, rb_ref, rc_ref)
    q_ref[...] = _bf16(q * (HEAD_DIM ** -0.5))
    k_ref[...] = _rope_lanes(_dot(xb, w_ref[:, a:2 * a]), ra_ref, rb_ref, rc_ref)
    v_ref[...] = _dot(xb, w_ref[:, 2 * a:3 * a])
    tail_ref[...] = _gated_conv(xb, w_ref, cw_ref, lambda u3: st_ref[...], cg_ref)


def _proj_long(x2d, w_in_b, w_qkvt_b, lane_tabs, row_tabs, conv_w, state8, bsz, seq):
    rows, d = x2d.shape
    tm = ROW_TILE
    nt = seq // tm
    a, c = ATTN_WIDTH, conv_w.shape[1]
    nblk = tm // MOBA_BLOCK
    row_blk = lambda b, t: (b * nt + t, 0)
    tab_spec = pl.BlockSpec((tm, LANES), lambda b, t: (t, 0))
    rtab_spec = pl.BlockSpec((ROT_DIM // 2, tm), lambda b, t: (0, t))
    st_spec = pl.BlockSpec((1, SUBLANES, c), lambda b, t: (b, 0, 0))
    kt_spec = pl.BlockSpec((1, a, tm), lambda b, t: (b, 0, t))
    vtb_spec = pl.BlockSpec((1, nblk, a, MOBA_BLOCK), lambda b, t: (b, t, 0, 0))
    f32, bf = jnp.float32, jnp.bfloat16
    out_shape = (
        jax.ShapeDtypeStruct((bsz, a, seq), bf),
        jax.ShapeDtypeStruct((rows, a), bf),
        jax.ShapeDtypeStruct((bsz, a, seq), f32), jax.ShapeDtypeStruct((bsz, a, seq), f32),
        jax.ShapeDtypeStruct((bsz, seq // MOBA_BLOCK, a, MOBA_BLOCK), bf),
        jax.ShapeDtypeStruct((rows, c), bf),
        jax.ShapeDtypeStruct(state8.shape, f32),
    )
    return pl.pallas_call(
        _proj_long_kernel,
        grid=(bsz, nt),
        in_specs=[pl.BlockSpec((tm, d), row_blk), _resident(w_in_b), _resident(w_qkvt_b),
                  tab_spec, tab_spec, tab_spec, rtab_spec, rtab_spec, _resident(conv_w), st_spec],
        out_specs=[kt_spec, pl.BlockSpec((tm, a), row_blk), kt_spec, kt_spec, vtb_spec,
                   pl.BlockSpec((tm, c), row_blk), st_spec],
        out_shape=out_shape,
        scratch_shapes=[pltpu.VMEM((SUBLANES, c), f32)],
        compiler_params=pltpu.CompilerParams(
            dimension_semantics=("arbitrary", "arbitrary"), vmem_limit_bytes=VMEM_LIMIT),
        name="proj_long",
    )(x2d, w_in_b, w_qkvt_b, *lane_tabs, *row_tabs, conv_w, state8)


def _proj_short(x2d, w_in_b, lane_tabs, conv_w, state8):
    rows, d = x2d.shape
    tm = ROW_TILE_SHORT
    a, c = ATTN_WIDTH, conv_w.shape[1]
    row_blk = lambda t: (t, 0)
    tab_spec = pl.BlockSpec((tm, LANES), lambda t: (0, 0))
    st_spec = pl.BlockSpec((tm // SUBLANES, SUBLANES, c), lambda t: (t, 0, 0))
    f32, bf = jnp.float32, jnp.bfloat16
    out_shape = (
        jax.ShapeDtypeStruct((rows, a), bf), jax.ShapeDtypeStruct((rows, a), f32),
        jax.ShapeDtypeStruct((rows, a), f32), jax.ShapeDtypeStruct((rows, c), bf),
        jax.ShapeDtypeStruct(state8.shape, f32),
    )
    return pl.pallas_call(
        _proj_short_kernel,
        grid=(rows // tm,),
        in_specs=[pl.BlockSpec((tm, d), row_blk), _resident(w_in_b),
                  tab_spec, tab_spec, tab_spec, _resident(conv_w), st_spec],
        out_specs=[pl.BlockSpec((tm, a), row_blk)] * 3 + [pl.BlockSpec((tm, c), row_blk), st_spec],
        out_shape=out_shape,
        compiler_params=pltpu.CompilerParams(
            dimension_semantics=("arbitrary",), vmem_limit_bytes=VMEM_LIMIT),
        name="proj_short",
    )(x2d, w_in_b, *lane_tabs, conv_w, state8)


def _rope_angles(pos):
    half = ROT_DIM // 2
    inv = 1.0 / (ROPE_THETA ** (jnp.arange(half, dtype=jnp.float32) / half))
    ang = pos.astype(jnp.float32)[:, None] * inv[None, :]
    return jnp.cos(ang), jnp.sin(ang)


def _rope_lane_tables(cos, sin):
    t, half = cos.shape
    rest = HEAD_DIM - ROT_DIM
    ones, zeros = jnp.ones((t, rest), jnp.float32), jnp.zeros((t, rest), jnp.float32)
    zh = jnp.zeros((t, half), jnp.float32)
    ta = jnp.concatenate([cos, cos, ones], axis=1)
    tb = jnp.concatenate([-sin, zh, zeros], axis=1)
    tc = jnp.concatenate([zh, sin, zeros], axis=1)
    return tuple(jnp.tile(x, (1, LANES // HEAD_DIM)) for x in (ta, tb, tc))


def _top_mask(gate, n_valid, axis):
    idx = lax.broadcasted_iota(jnp.int32, gate.shape, axis)
    n = gate.shape[axis]
    g = jnp.where(idx < n_valid, gate, -jnp.inf)
    sel = jnp.zeros(gate.shape, jnp.bool_)
    for _ in range(MOBA_TOPK):
        mx = jnp.max(g, axis=axis, keepdims=True)
        first = jnp.min(jnp.where(g == mx, idx, n), axis=axis, keepdims=True)
        pick = (idx == first) & (mx > -jnp.inf)
        sel = sel | pick
        g = jnp.where(pick, -jnp.inf, g)
    return sel


def _moba_prompt_kernel(qt_ref, k_ref, vt_ref, o_ref, km_ref, m_ref, acc_ref):
    i = pl.program_id(2)
    blk = MOBA_BLOCK
    nb = vt_ref.shape[1]
    nbp = -(-(nb + 1) // BF16_ROWS) * BF16_ROWS
    f32, bf = jnp.float32, jnp.bfloat16

    @pl.when(i == 0)
    def _():
        kf = k_ref[0].astype(f32).reshape(nb, blk, LANES)
        km_ref[...] = jnp.zeros(km_ref.shape, f32)
        km_ref[0:nb, :] = jnp.sum(kf, axis=1) * (1.0 / blk)

    row_q = lax.broadcasted_iota(jnp.int32, (LANES, blk), 0)
    head0 = _bf16(jnp.where(row_q < HEAD_DIM, 1.0, 0.0))
    head1 = _bf16(jnp.where(row_q < HEAD_DIM, 0.0, 1.0))
    qt = qt_ref[0]
    q2 = jnp.concatenate([qt * head0, qt * head1], axis=1)
    gate = _dot(km_ref[...], q2.astype(f32))[0:nbp]
    row_g = lax.broadcasted_iota(jnp.int32, gate.shape, 0)
    visible = _top_mask(gate, i, 0) | (row_g == i)
    bias = _bf16(jnp.where(visible, 0.0, MASK_BIAS))
    q_aug = jnp.concatenate([q2, bias, jnp.zeros((LANES - nbp, 2 * blk), bf)], axis=0)

    lane_k = lax.broadcasted_iota(jnp.int32, (blk, LANES), 1)
    ones_rows = jnp.ones((BF16_ROWS, blk), bf)

    def block(j, bias_row):
        start = pl.multiple_of(j * blk, blk)
        k_aug = jnp.concatenate([k_ref[0, pl.ds(start, blk), :],
                                 _bf16(jnp.where(lane_k == bias_row, 1.0, 0.0))], axis=1)
        st = _dot(k_aug, q_aug)
        vv = jnp.concatenate([vt_ref[0, j], ones_rows], axis=0)
        return st, vv

    def fold(sts, vvs, first):
        m_new = functools.reduce(jnp.maximum, [jnp.max(st, axis=0, keepdims=True) for st in sts])
        if not first:
            m_new = jnp.maximum(m_ref[...], m_new)
        pv = functools.reduce(jnp.add, [_dot(vv, _bf16(jnp.exp(st - m_new))) for st, vv in zip(sts, vvs)])
        if first:
            acc_ref[...] = pv
        else:
            acc_ref[...] = acc_ref[...] * jnp.exp(m_ref[...] - m_new) + pv
        m_ref[...] = m_new

    st, vv = block(i, i)
    key = lax.broadcasted_iota(jnp.int32, st.shape, 0)
    col = lax.broadcasted_iota(jnp.int32, st.shape, 1)
    qpos = jnp.where(col >= blk, col - blk, col)
    fold([jnp.where(key <= qpos, st, -jnp.inf)], [vv], True)

    def body(t, carry):
        j0, j1 = 2 * t, 2 * t + 1
        st0, vv0 = block(j0, j0)
        st1, vv1 = block(jnp.minimum(j1, i - 1), jnp.where(j1 < i, j1, nb))
        fold([st0, st1], [vv0, vv1], False)
        return carry

    lax.fori_loop(0, (i + 1) // 2, body, 0)

    acc = acc_ref[...]
    ot = acc[0:LANES] / acc[LANES:LANES + 1]
    x = jnp.concatenate([ot[0:HEAD_DIM, 0:blk], ot[HEAD_DIM:LANES, blk:2 * blk]], axis=0)
    o_ref[0] = _bf16(x.T)


def _moba_prompt(qt, k_rm, vtb, bsz, seq):
    blk = MOBA_BLOCK
    nb = seq // blk
    assert nb + 1 <= LANES
    k3 = k_rm.reshape(bsz, seq, ATTN_WIDTH)
    qt_spec = pl.BlockSpec((1, LANES, blk), lambda b, h, i: (b, h, i))
    k_spec = pl.BlockSpec((1, seq, LANES), lambda b, h, i: (b, 0, h))
    vt_spec = pl.BlockSpec((1, nb, LANES, blk), lambda b, h, i: (b, 0, h, 0))
    out = pl.pallas_call(
        _moba_prompt_kernel,
        grid=(bsz, ATTN_WIDTH // LANES, nb),
        in_specs=[qt_spec, k_spec, vt_spec],
        out_specs=pl.BlockSpec((1, blk, LANES), lambda b, h, i: (b, i, h)),
        out_shape=jax.ShapeDtypeStruct((bsz, seq, ATTN_WIDTH), jnp.bfloat16),
        scratch_shapes=[pltpu.VMEM((LANES, LANES), jnp.float32),
                        pltpu.VMEM((1, 2 * blk), jnp.float32),
                        pltpu.VMEM((LANES + BF16_ROWS, 2 * blk), jnp.float32)],
        compiler_params=pltpu.CompilerParams(
            dimension_semantics=("arbitrary", "arbitrary", "arbitrary"), vmem_limit_bytes=VMEM_LIMIT),
        name="moba_prompt",
    )(qt, k3, vtb)
    return out.reshape(bsz * seq, ATTN_WIDTH)


def _moba_sample_kernel(pt_ref, q_ref, k1_ref, v1_ref, *rest):
    del pt_ref
    n_pg = PAGES_PER_STEP
    k_pages, v_pages = rest[:n_pg], rest[n_pg:2 * n_pg]
    o_ref, wq_ref, s_ref, p_ref, l_ref, acc_ref = rest[2 * n_pg:]
    g = pl.program_id(1)
    ks = s_ref.shape[0]
    f32 = jnp.float32
    t_new = q_ref.shape[1]
    rows = N_HEADS * t_new
    page = k_pages[0].shape[1]
    blk = MOBA_BLOCK
    blks_per_step = n_pg * page // blk
    row = lax.broadcasted_iota(jnp.int32, (rows, ATTN_WIDTH), 0)
    lane = lax.broadcasted_iota(jnp.int32, (rows, ATTN_WIDTH), 1)
    own_head = (row // t_new) == (lane // HEAD_DIM)

    @pl.when(g == 0)
    def _():
        q8 = q_ref[0].astype(f32)
        wq_ref[...] = _bf16(jnp.where(own_head, jnp.concatenate([q8] * N_HEADS, axis=0), 0.0))

    @pl.when(g < ks)
    def _():
        wq = wq_ref[...]
        for p in range(n_pg):
            s_ref[g, :, p * page:(p + 1) * page] = _dot(wq, _bf16(k_pages[p][...]))

    @pl.when(g == ks - 1)
    def _():
        wq = wq_ref[...]
        nblk = ks * blks_per_step
        blocks = [(n // blks_per_step, (n % blks_per_step) * blk) for n in range(nblk)]
        lane_g = lax.broadcasted_iota(jnp.int32, (rows, LANES), 1)
        gate = jnp.zeros((rows, LANES), f32)
        for n, (gi, c0) in enumerate(blocks):
            gate = jnp.where(lane_g == n, jnp.sum(s_ref[gi, :, c0:c0 + blk], axis=-1, keepdims=True), gate)
        bias = jnp.where(_top_mask(gate, nblk, 1), 0.0, -jnp.inf)
        zpad = jnp.zeros((LANES - t_new, ATTN_WIDTH), jnp.bfloat16)
        k_own = jnp.concatenate([_bf16(k1_ref[0]), zpad], axis=0)
        v_own = jnp.concatenate([_bf16(v1_ref[0]), zpad], axis=0)
        s_own = _dot_nt(wq, k_own)
        r_o = lax.broadcasted_iota(jnp.int32, s_own.shape, 0)
        c_o = lax.broadcasted_iota(jnp.int32, s_own.shape, 1)
        s_own = jnp.where(c_o <= r_o % t_new, s_own, -jnp.inf)
        run_max = jnp.full((rows, blk), -jnp.inf, f32)
        for n, (gi, c0) in enumerate(blocks):
            run_max = jnp.maximum(run_max, s_ref[gi, :, c0:c0 + blk] + bias[:, n:n + 1])
        m = jnp.maximum(jnp.max(run_max, axis=-1, keepdims=True), jnp.max(s_own, axis=-1, keepdims=True))
        den = jnp.zeros((rows, blk), f32)
        for n, (gi, c0) in enumerate(blocks):
            p_blk = jnp.exp(s_ref[gi, :, c0:c0 + blk] + bias[:, n:n + 1] - m)
            den = den + p_blk
            p_ref[gi, :, c0:c0 + blk] = _bf16(p_blk)
        p_own = jnp.exp(s_own - m)
        l_ref[...] = jnp.sum(den, axis=-1, keepdims=True) + jnp.sum(p_own, axis=-1, keepdims=True)
        acc_ref[...] = _dot(_bf16(p_own), v_own)

    @pl.when(g >= ks)
    def _():
        gv = g - ks
        o = acc_ref[...]
        for p in range(n_pg):
            o = o + _dot_nt(p_ref[gv, :, p * page:(p + 1) * page], _bf16(v_pages[p][...]))
        acc_ref[...] = o

    @pl.when(g == 2 * ks - 1)
    def _():
        o64 = jnp.where(own_head, acc_ref[...] / l_ref[...], 0.0)
        o8 = o64[0:t_new]
        for h in range(1, N_HEADS):
            o8 = o8 + o64[h * t_new:(h + 1) * t_new]
        o_ref[0] = _bf16(o8)


def _moba_sample(qb, k_new, v_new, cache_k, cache_v, page_table, n_seq, t_new):
    n_phys, page = cache_k.shape[0], cache_k.shape[1]
    ck = cache_k.transpose(0, 2, 3, 1).reshape(n_phys, ATTN_WIDTH, page)
    cv = cache_v.transpose(0, 2, 3, 1).reshape(n_phys, ATTN_WIDTH, page)
    n_pages = page_table.shape[1]
    pps = PAGES_PER_STEP
    ks = n_pages // pps
    rows = N_HEADS * t_new
    seq_spec = pl.BlockSpec((1, t_new, ATTN_WIDTH), lambda s, g, pt: (s, 0, 0))

    def k_page_spec(p):
        return pl.BlockSpec((None, ATTN_WIDTH, page),
                            lambda s, g, pt: (pt[s, jnp.minimum(g, ks - 1) * pps + p], 0, 0))

    def v_page_spec(p):
        def index(s, g, pt):
            in_v = g >= ks
            return pt[jnp.where(in_v, s, jnp.maximum(s - 1, 0)), jnp.where(in_v, g - ks, ks - 1) * pps + p], 0, 0
        return pl.BlockSpec((None, ATTN_WIDTH, page), index)

    f32, bf = jnp.float32, jnp.bfloat16
    out = pl.pallas_call(
        _moba_sample_kernel,
        grid_spec=pltpu.PrefetchScalarGridSpec(
            num_scalar_prefetch=1,
            grid=(n_seq, 2 * ks),
            in_specs=([seq_spec, seq_spec, seq_spec] + [k_page_spec(p) for p in range(pps)]
                      + [v_page_spec(p) for p in range(pps)]),
            out_specs=seq_spec,
            scratch_shapes=[pltpu.VMEM((rows, ATTN_WIDTH), bf),
                            pltpu.VMEM((ks, rows, pps * page), f32),
                            pltpu.VMEM((ks, rows, pps * page), bf),
                            pltpu.VMEM((rows, 1), f32),
                            pltpu.VMEM((rows, ATTN_WIDTH), f32)]),
        out_shape=jax.ShapeDtypeStruct((n_seq, t_new, ATTN_WIDTH), bf),
        compiler_params=pltpu.CompilerParams(
            dimension_semantics=("arbitrary", "arbitrary"), vmem_limit_bytes=VMEM_LIMIT),
        name="moba_sample",
    )(page_table, qb.reshape(n_seq, t_new, ATTN_WIDTH), k_new.reshape(n_seq, t_new, ATTN_WIDTH),
      v_new.reshape(n_seq, t_new, ATTN_WIDTH), *([ck] * pps), *([cv] * pps))
    return out.reshape(n_seq * t_new, ATTN_WIDTH)


def _post_kernel(x_ref, at_ref, cg_ref, st_ref, wo_ref, g1_ref, b1_ref, wup_ref, fw_ref, fb_ref, wdn_ref,
                 g2_ref, b2_ref, y_ref, tail_ref, carry_ref, *, long_seq, alpha):
    a = at_ref.shape[1]
    rows = x_ref.shape[0]
    d_ff = wdn_ref.shape[0]
    mix = _dot(at_ref[...], wo_ref[0:a, :]) + _dot(cg_ref[...], wo_ref[a:, :])
    x1 = _layer_norm(alpha * x_ref[...] + mix, g1_ref[...], b1_ref[...])
    x1b = _bf16(x1)

    if long_seq:
        @pl.when(pl.program_id(1) == 0)
        def _():
            carry_ref[...] = st_ref[0]

    f = jnp.zeros(x1.shape, jnp.float32)
    for c0 in range(0, d_ff, FF_CHUNK):
        cols = slice(c0, c0 + FF_CHUNK)
        up_a = _dot(x1b, wup_ref[:, cols])
        up_g = _dot(x1b, wup_ref[:, d_ff + c0:d_ff + c0 + FF_CHUNK])
        a3 = up_a.reshape(rows // SUBLANES, SUBLANES, FF_CHUNK)
        if long_seq:
            prev3 = _prev_groups(a3, carry_ref[:, cols])
            carry_ref[:, cols] = a3[-1]
            tail_ref[0, :, cols] = a3[-1]
        else:
            prev3 = st_ref[:, :, cols]
            tail_ref[:, :, cols] = a3
        a_c = (_causal_conv3(a3, prev3, fw_ref, cols) + fb_ref[0:1, cols]).reshape(rows, FF_CHUNK)
        act = 0.5 * a_c * (1.0 + lax.erf(a_c * np.float32(np.sqrt(0.5))))
        f = f + _dot(_bf16(act * up_g), wdn_ref[cols, :])
    y_ref[...] = _layer_norm(alpha * x1 + f, g2_ref[...], b2_ref[...])


def _post(x2d, attn, cg, state8, w_o_b, ln1_g, ln1_b, w_up_b, fcw, fcb, w_dn_b, ln2_g, ln2_b,
          *, nb, nt, long_seq, alpha):
    rows, d = x2d.shape
    tm = rows // (nb * nt)
    d_ff = w_dn_b.shape[0]
    row_blk = lambda b, t: (b * nt + t, 0)
    if long_seq:
        st_spec = pl.BlockSpec((1, SUBLANES, d_ff), lambda b, t: (b, 0, 0))
    else:
        st_spec = pl.BlockSpec((tm // SUBLANES, SUBLANES, d_ff), lambda b, t: (t, 0, 0))
    vec = lambda v: v.reshape(1, -1)
    ln1_g, ln1_b, fcb, ln2_g, ln2_b = (vec(v) for v in (ln1_g, ln1_b, fcb, ln2_g, ln2_b))
    consts = (w_o_b, ln1_g, ln1_b, w_up_b, fcw, fcb, w_dn_b, ln2_g, ln2_b)
    return pl.pallas_call(
        functools.partial(_post_kernel, long_seq=long_seq, alpha=alpha),
        grid=(nb, nt),
        in_specs=[pl.BlockSpec((tm, d), row_blk),
                  pl.BlockSpec((tm, attn.shape[1]), row_blk),
                  pl.BlockSpec((tm, cg.shape[1]), row_blk),
                  st_spec] + [_resident(w) for w in consts],
        out_specs=[pl.BlockSpec((tm, d), row_blk), st_spec],
        out_shape=(jax.ShapeDtypeStruct((rows, d), jnp.float32),
                   jax.ShapeDtypeStruct(state8.shape, jnp.float32)),
        scratch_shapes=[pltpu.VMEM((SUBLANES, d_ff), jnp.float32)],
        compiler_params=pltpu.CompilerParams(
            dimension_semantics=("arbitrary", "arbitrary"), vmem_limit_bytes=VMEM_LIMIT),
        name="post_long" if long_seq else "post_short",
    )(x2d, attn, cg, state8, *consts)


def _pad_state(state):
    return jnp.pad(state, ((0, 0), (SUBLANES - (CONV_K - 1), 0), (0, 0)))


def kernel(x_prompt, x_sample, cache_k, cache_v, page_table, state_conv, state_ffn_conv, w_in, conv_w, w_o,
           ln1_g, ln1_b, w_up, ffn_conv_w, ffn_conv_b, w_down, ln2_g, ln2_b):
    depth = w_in.shape[0]
    bsz, seq, d_model = x_prompt.shape
    n_seq, t_new, _ = x_sample.shape
    conv_width = conv_w.shape[2]
    d_ff = w_down.shape[1]
    past_len = page_table.shape[1] * cache_k.shape[2]
    alpha = (2.0 * depth) ** 0.25
    a = ATTN_WIDTH
    assert t_new == SUBLANES and seq % ROW_TILE == 0 and (n_seq * t_new) % ROW_TILE_SHORT == 0
    assert past_len % MOBA_BLOCK == 0 and past_len // MOBA_BLOCK >= MOBA_TOPK and d_ff % FF_CHUNK == 0
    assert page_table.shape[1] % PAGES_PER_STEP == 0 and (PAGES_PER_STEP * cache_k.shape[2]) % MOBA_BLOCK == 0

    cos_p, sin_p = _rope_angles(jnp.arange(seq, dtype=jnp.int32))
    cos_s, sin_s = _rope_angles(past_len + jnp.arange(t_new, dtype=jnp.int32))
    lane_tabs_p = _rope_lane_tables(cos_p, sin_p)
    row_tabs_p = (cos_p.T, sin_p.T)
    lane_tabs_s = tuple(jnp.tile(t, (ROW_TILE_SHORT // t_new, 1)) for t in _rope_lane_tables(cos_s, sin_s))
    grid_p = dict(nb=bsz, nt=seq // ROW_TILE, long_seq=True)
    grid_s = dict(nb=1, nt=n_seq * t_new // ROW_TILE_SHORT, long_seq=False)
    zeros_conv = jnp.zeros((bsz, SUBLANES, conv_width), jnp.float32)
    zeros_ffn = jnp.zeros((bsz, SUBLANES, d_ff), jnp.float32)

    xp = x_prompt.reshape(bsz * seq, d_model)
    xs = x_sample.reshape(n_seq * t_new, d_model)
    outs = [[] for _ in range(8)]
    for l in range(depth):
        w_in_b, w_o_b, w_up_b, w_dn_b = _bf16(w_in[l]), _bf16(w_o[l]), _bf16(w_up[l]), _bf16(w_down[l])
        w_qkvt_b = w_in_b[:, 0:3 * a].T
        post_w = (w_o_b, ln1_g[l], ln1_b[l], w_up_b, ffn_conv_w[l], ffn_conv_b[l], w_dn_b, ln2_g[l], ln2_b[l])

        qt, k_rm, kt, vt, vtb, cg, tail = _proj_long(
            xp, w_in_b, w_qkvt_b, lane_tabs_p, row_tabs_p, conv_w[l], zeros_conv, bsz, seq)
        attn = _moba_prompt(qt, k_rm, vtb, bsz, seq)
        xp, ffn_tail = _post(xp, attn, cg, zeros_ffn, *post_w, alpha=alpha, **grid_p)
        new = (kt.reshape(bsz, N_HEADS, HEAD_DIM, seq).transpose(0, 3, 1, 2),
               vt.reshape(bsz, N_HEADS, HEAD_DIM, seq).transpose(0, 3, 1, 2),
               tail[:, -(CONV_K - 1):], ffn_tail[:, -(CONV_K - 1):])
        for dst, val in zip(outs[0::2], new):
            dst.append(val)

        qb, k, v, cg, tail = _proj_short(xs, w_in_b, lane_tabs_s, conv_w[l], _pad_state(state_conv[l]))
        attn = _moba_sample(qb, k, v, cache_k[l], cache_v[l], page_table, n_seq, t_new)
        xs, ffn_tail = _post(xs, attn, cg, _pad_state(state_ffn_conv[l]), *post_w, alpha=alpha, **grid_s)
        new = (k.reshape(n_seq, t_new, N_HEADS, HEAD_DIM), v.reshape(n_seq, t_new, N_HEADS, HEAD_DIM),
               tail[:, -(CONV_K - 1):], ffn_tail[:, -(CONV_K - 1):])
        for dst, val in zip(outs[1::2], new):
            dst.append(val)

    kp, ks, vp, vs, cp, cs, fp, fs = (jnp.stack(o) for o in outs)
    return (xp.reshape(bsz, seq, d_model), xs.reshape(n_seq, t_new, d_model), kp, vp, ks, vs, cp, cs, fp, fs)
```

```python
import functools

import jax
import jax.numpy as jnp
import numpy as np
from jax import lax
from jax.experimental import pallas as pl
from jax.experimental.pallas import tpu as pltpu

N_HEADS = 8
HEAD_DIM = 64
ATTN_WIDTH = N_HEADS * HEAD_DIM
ROT_DIM = HEAD_DIM // 4
ROPE_THETA = 500000.0
MOBA_BLOCK = 256
MOBA_TOPK = 3
CONV_K = 3
LN_EPS = 1e-5

LANES = 128
SUBLANES = 8
BF16_ROWS = 16
ROW_TILE = 512
ROW_TILE_SHORT = 128
FF_CHUNK = 256
PAGES_PER_STEP = 16
PAST_BLOCKS_PER_TRIP = 4
VMEM_LIMIT = 56 * 1024 * 1024
MASK_BIAS = -1e30

_NT = (((1,), (1,)), ((), ()))


def _bf16(x):
    return x.astype(jnp.bfloat16)


def _dot(a, b):
    return jnp.dot(a, b, preferred_element_type=jnp.float32)


def _dot_nt(a, b):
    return lax.dot_general(a, b, _NT, preferred_element_type=jnp.float32)


def _layer_norm(x, g, b):
    mu = jnp.mean(x, axis=-1, keepdims=True)
    xc = x - mu
    var = jnp.mean(xc * xc, axis=-1, keepdims=True)
    return xc * lax.rsqrt(var + LN_EPS) * g + b


def _resident(arr):
    return pl.BlockSpec(arr.shape, lambda *_: (0,) * arr.ndim, pipeline_mode=pl.Buffered(1))


def _causal_conv3(u3, prev3, w_ref, cols):
    sub = lax.broadcasted_iota(jnp.int32, u3.shape, 1)
    s1 = jnp.where(sub < 1, pltpu.roll(prev3, 1, 1), pltpu.roll(u3, 1, 1))
    s2 = jnp.where(sub < 2, pltpu.roll(prev3, 2, 1), pltpu.roll(u3, 2, 1))
    return s2 * w_ref[0:1, cols] + s1 * w_ref[1:2, cols] + u3 * w_ref[2:3, cols]


def _prev_groups(u3, carry):
    return jnp.concatenate([carry[None], u3[:-1]], axis=0)


def _rope_lanes(z, ra_ref, rb_ref, rc_ref):
    outs = []
    for c in range(z.shape[1] // LANES):
        zc = z[:, c * LANES:(c + 1) * LANES]
        outs.append(zc * ra_ref[...]
                    + pltpu.roll(zc, LANES - ROT_DIM // 2, 1) * rb_ref[...]
                    + pltpu.roll(zc, ROT_DIM // 2, 1) * rc_ref[...])
    return jnp.concatenate(outs, axis=1)


def _rope_rows(zt, cos_ref, sin_ref):
    half = ROT_DIM // 2
    cos, sin = cos_ref[...], sin_ref[...]
    parts = []
    for h in range(N_HEADS):
        r0 = h * HEAD_DIM
        x1, x2 = zt[r0:r0 + half], zt[r0 + half:r0 + ROT_DIM]
        parts += [x1 * cos - x2 * sin, x2 * cos + x1 * sin, zt[r0 + ROT_DIM:r0 + HEAD_DIM]]
    return jnp.concatenate(parts, axis=0)


def _gated_conv(xb, w_ref, cw_ref, prev3_fn, cg_ref):
    a = ATTN_WIDTH
    c = cw_ref.shape[1]
    rows = xb.shape[0]
    gb = _dot(xb, w_ref[:, 3 * a:3 * a + c])
    u = _dot(xb, w_ref[:, 3 * a + c:3 * a + 2 * c]) * _dot(xb, w_ref[:, 3 * a + 2 * c:3 * a + 3 * c])
    u3 = u.reshape(rows // SUBLANES, SUBLANES, c)
    y = _causal_conv3(u3, prev3_fn(u3), cw_ref, slice(None)).reshape(rows, c)
    cg_ref[...] = _bf16(gb * y)
    return u3


def _proj_long_kernel(x_ref, w_ref, wt_ref, ra_ref, rb_ref, rc_ref, cos_ref, sin_ref, cw_ref, st_ref,
                      qt_ref, k_ref, kt_ref, vt_ref, vtb_ref, cg_ref, tail_ref, carry_ref):
    a = ATTN_WIDTH
    xb = _bf16(x_ref[...])
    qkvt = _dot_nt(wt_ref[...], xb)
    qt_ref[0] = _bf16(_rope_rows(qkvt[0:a], cos_ref, sin_ref) * (HEAD_DIM ** -0.5))
    kt_ref[0] = _rope_rows(qkvt[a:2 * a], cos_ref, sin_ref)
    vt = qkvt[2 * a:3 * a]
    vt_ref[0] = vt
    for c in range(vtb_ref.shape[1]):
        vtb_ref[0, c] = _bf16(vt[:, c * MOBA_BLOCK:(c + 1) * MOBA_BLOCK])
    k_ref[...] = _bf16(_rope_lanes(_dot(xb, w_ref[:, a:2 * a]), ra_ref, rb_ref, rc_ref))

    @pl.when(pl.program_id(1) == 0)
    def _():
        carry_ref[...] = st_ref[0]

    u3 = _gated_conv(xb, w_ref, cw_ref, lambda u3: _prev_groups(u3, carry_ref[...]), cg_ref)
    carry_ref[...] = u3[-1]
    tail_ref[0] = u3[-1]


def _proj_short_kernel(x_ref, w_ref, ra_ref, rb_ref, rc_ref, cw_ref, st_ref,
                       q_ref, k_ref, v_ref, cg_ref, tail_ref):
    a = ATTN_WIDTH
    xb = _bf16(x_ref[...])
    q = _rope_lanes(_dot(xb, w_ref[:, 0:a]), ra_ref, rb_ref, rc_ref)
    q_ref[...] = _bf16(q * (HEAD_DIM ** -0.5))
    k_ref[...] = _rope_lanes(_dot(xb, w_ref[:, a:2 * a]), ra_ref, rb_ref, rc_ref)
    v_ref[...] = _dot(xb, w_ref[:, 2 * a:3 * a])
    tail_ref[...] = _gated_conv(xb, w_ref, cw_ref, lambda u3: st_ref[...], cg_ref)


def _proj_long(x2d, w_in_b, w_qkvt_b, lane_tabs, row_tabs, conv_w, state8, bsz, seq):
    rows, d = x2d.shape
    tm = ROW_TILE
    nt = seq // tm
    a, c = ATTN_WIDTH, conv_w.shape[1]
    nblk = tm // MOBA_BLOCK
    row_blk = lambda b, t: (b * nt + t, 0)
    tab_spec = pl.BlockSpec((tm, LANES), lambda b, t: (t, 0))
    rtab_spec = pl.BlockSpec((ROT_DIM // 2, tm), lambda b, t: (0, t))
    st_spec = pl.BlockSpec((1, SUBLANES, c), lambda b, t: (b, 0, 0))
    kt_spec = pl.BlockSpec((1, a, tm), lambda b, t: (b, 0, t))
    vtb_spec = pl.BlockSpec((1, nblk, a, MOBA_BLOCK), lambda b, t: (b, t, 0, 0))
    f32, bf = jnp.float32, jnp.bfloat16
    out_shape = (
        jax.ShapeDtypeStruct((bsz, a, seq), bf),
        jax.ShapeDtypeStruct((rows, a), bf),
        jax.ShapeDtypeStruct((bsz, a, seq), f32), jax.ShapeDtypeStruct((bsz, a, seq), f32),
        jax.ShapeDtypeStruct((bsz, seq // MOBA_BLOCK, a, MOBA_BLOCK), bf),
        jax.ShapeDtypeStruct((rows, c), bf),
        jax.ShapeDtypeStruct(state8.shape, f32),
    )
    return pl.pallas_call(
        _proj_long_kernel,
        grid=(bsz, nt),
        in_specs=[pl.BlockSpec((tm, d), row_blk), _resident(w_in_b), _resident(w_qkvt_b),
                  tab_spec, tab_spec, tab_spec, rtab_spec, rtab_spec, _resident(conv_w), st_spec],
        out_specs=[kt_spec, pl.BlockSpec((tm, a), row_blk), kt_spec, kt_spec, vtb_spec,
                   pl.BlockSpec((tm, c), row_blk), st_spec],
        out_shape=out_shape,
        scratch_shapes=[pltpu.VMEM((SUBLANES, c), f32)],
        compiler_params=pltpu.CompilerParams(
            dimension_semantics=("arbitrary", "arbitrary"), vmem_limit_bytes=VMEM_LIMIT),
        name="proj_long",
    )(x2d, w_in_b, w_qkvt_b, *lane_tabs, *row_tabs, conv_w, state8)


def _proj_short(x2d, w_in_b, lane_tabs, conv_w, state8):
    rows, d = x2d.shape
    tm = ROW_TILE_SHORT
    a, c = ATTN_WIDTH, conv_w.shape[1]
    row_blk = lambda t: (t, 0)
    tab_spec = pl.BlockSpec((tm, LANES), lambda t: (0, 0))
    st_spec = pl.BlockSpec((tm // SUBLANES, SUBLANES, c), lambda t: (t, 0, 0))
    f32, bf = jnp.float32, jnp.bfloat16
    out_shape = (
        jax.ShapeDtypeStruct((rows, a), bf), jax.ShapeDtypeStruct((rows, a), f32),
        jax.ShapeDtypeStruct((rows, a), f32), jax.ShapeDtypeStruct((rows, c), bf),
        jax.ShapeDtypeStruct(state8.shape, f32),
    )
    return pl.pallas_call(
        _proj_short_kernel,
        grid=(rows // tm,),
        in_specs=[pl.BlockSpec((tm, d), row_blk), _resident(w_in_b),
                  tab_spec, tab_spec, tab_spec, _resident(conv_w), st_spec],
        out_specs=[pl.BlockSpec((tm, a), row_blk)] * 3 + [pl.BlockSpec((tm, c), row_blk), st_spec],
        out_shape=out_shape,
        compiler_params=pltpu.CompilerParams(
            dimension_semantics=("arbitrary",), vmem_limit_bytes=VMEM_LIMIT),
        name="proj_short",
    )(x2d, w_in_b, *lane_tabs, conv_w, state8)


def _rope_angles(pos):
    half = ROT_DIM // 2
    inv = 1.0 / (ROPE_THETA ** (jnp.arange(half, dtype=jnp.float32) / half))
    ang = pos.astype(jnp.float32)[:, None] * inv[None, :]
    return jnp.cos(ang), jnp.sin(ang)


def _rope_lane_tables(cos, sin):
    t, half = cos.shape
    rest = HEAD_DIM - ROT_DIM
    ones, zeros = jnp.ones((t, rest), jnp.float32), jnp.zeros((t, rest), jnp.float32)
    zh = jnp.zeros((t, half), jnp.float32)
    ta = jnp.concatenate([cos, cos, ones], axis=1)
    tb = jnp.concatenate([-sin, zh, zeros], axis=1)
    tc = jnp.concatenate([zh, sin, zeros], axis=1)
    return tuple(jnp.tile(x, (1, LANES // HEAD_DIM)) for x in (ta, tb, tc))


def _top_mask(gate, n_valid, axis):
    idx = lax.broadcasted_iota(jnp.int32, gate.shape, axis)
    n = gate.shape[axis]
    g = jnp.where(idx < n_valid, gate, -jnp.inf)
    sel = jnp.zeros(gate.shape, jnp.bool_)
    for _ in range(MOBA_TOPK):
        mx = jnp.max(g, axis=axis, keepdims=True)
        first = jnp.min(jnp.where(g == mx, idx, n), axis=axis, keepdims=True)
        pick = (idx == first) & (mx > -jnp.inf)
        sel = sel | pick
        g = jnp.where(pick, -jnp.inf, g)
    return sel


def _moba_prompt_kernel(qt_ref, k_ref, vt_ref, o_ref, km_ref, m_ref, acc_ref):
    i = pl.program_id(2)
    blk = MOBA_BLOCK
    nb = vt_ref.shape[1]
    nbp = -(-nb // BF16_ROWS) * BF16_ROWS
    f32, bf = jnp.float32, jnp.bfloat16

    @pl.when(i == 0)
    def _():
        kf = k_ref[0].astype(f32).reshape(nb, blk, LANES)
        km_ref[...] = jnp.zeros(km_ref.shape, f32)
        km_ref[0:nb, :] = jnp.sum(kf, axis=1) * (1.0 / blk)

    row_q = lax.broadcasted_iota(jnp.int32, (LANES, blk), 0)
    head0 = _bf16(jnp.where(row_q < HEAD_DIM, 1.0, 0.0))
    head1 = _bf16(jnp.where(row_q < HEAD_DIM, 0.0, 1.0))
    qt = qt_ref[0]
    q2 = jnp.concatenate([qt * head0, qt * head1], axis=1)
    gate = _dot(km_ref[...], q2.astype(f32))[0:nbp]
    row_g = lax.broadcasted_iota(jnp.int32, gate.shape, 0)
    visible = _top_mask(gate, i, 0) | (row_g == i)
    bias = _bf16(jnp.where(visible, 0.0, MASK_BIAS))
    q_aug = jnp.concatenate([q2, bias, jnp.zeros((LANES - nbp, 2 * blk), bf)], axis=0)

    lane_k = lax.broadcasted_iota(jnp.int32, (blk, LANES), 1)
    ones_rows = jnp.ones((BF16_ROWS, blk), bf)

    def block(j):
        start = pl.multiple_of(j * blk, blk)
        k_aug = jnp.concatenate([k_ref[0, pl.ds(start, blk), :],
                                 _bf16(jnp.where(lane_k == j, 1.0, 0.0))], axis=1)
        st = _dot(k_aug, q_aug)
        vv = jnp.concatenate([vt_ref[0, j], ones_rows], axis=0)
        return st, vv

    def fold(sts, vvs, first):
        m_new = functools.reduce(jnp.maximum, [jnp.max(st, axis=0, keepdims=True) for st in sts])
        if not first:
            m_new = jnp.maximum(m_ref[...], m_new)
        pv = functools.reduce(jnp.add, [_dot(vv, _bf16(jnp.exp(st - m_new))) for st, vv in zip(sts, vvs)])
        if first:
            acc_ref[...] = pv
        else:
            acc_ref[...] = acc_ref[...] * jnp.exp(m_ref[...] - m_new) + pv
        m_ref[...] = m_new

    st, vv = block(i)
    key = lax.broadcasted_iota(jnp.int32, st.shape, 0)
    col = lax.broadcasted_iota(jnp.int32, st.shape, 1)
    qpos = jnp.where(col >= blk, col - blk, col)
    fold([jnp.where(key <= qpos, st, -jnp.inf)], [vv], True)

    def fold_blocks(js):
        sts, vvs = zip(*[block(j) for j in js])
        fold(sts, vvs, False)

    per_trip = PAST_BLOCKS_PER_TRIP

    def body(t, carry):
        fold_blocks([per_trip * t + u for u in range(per_trip)])
        return carry

    lax.fori_loop(0, i // per_trip, body, 0)
    done = i // per_trip * per_trip
    width = per_trip // 2
    while width:
        @pl.when(((i - done) & width) != 0)
        def _(width=width):
            first = done + ((i - done) & ~(2 * width - 1))
            fold_blocks([first + u for u in range(width)])
        width //= 2

    acc = acc_ref[...]
    ot = acc[0:LANES] / acc[LANES:LANES + 1]
    x = jnp.concatenate([ot[0:HEAD_DIM, 0:blk], ot[HEAD_DIM:LANES, blk:2 * blk]], axis=0)
    o_ref[0] = _bf16(x.T)


def _moba_prompt(qt, k_rm, vtb, bsz, seq):
    blk = MOBA_BLOCK
    nb = seq // blk
    assert nb <= LANES
    k3 = k_rm.reshape(bsz, seq, ATTN_WIDTH)
    qt_spec = pl.BlockSpec((1, LANES, blk), lambda b, h, i: (b, h, i))
    k_spec = pl.BlockSpec((1, seq, LANES), lambda b, h, i: (b, 0, h))
    vt_spec = pl.BlockSpec((1, nb, LANES, blk), lambda b, h, i: (b, 0, h, 0))
    out = pl.pallas_call(
        _moba_prompt_kernel,
        grid=(bsz, ATTN_WIDTH // LANES, nb),
        in_specs=[qt_spec, k_spec, vt_spec],
        out_specs=pl.BlockSpec((1, blk, LANES), lambda b, h, i: (b, i, h)),
        out_shape=jax.ShapeDtypeStruct((bsz, seq, ATTN_WIDTH), jnp.bfloat16),
        scratch_shapes=[pltpu.VMEM((LANES, LANES), jnp.float32),
                        pltpu.VMEM((1, 2 * blk), jnp.float32),
                        pltpu.VMEM((LANES + BF16_ROWS, 2 * blk), jnp.float32)],
        compiler_params=pltpu.CompilerParams(
            dimension_semantics=("arbitrary", "arbitrary", "arbitrary"), vmem_limit_bytes=VMEM_LIMIT),
        name="moba_prompt",
    )(qt, k3, vtb)
    return out.reshape(bsz * seq, ATTN_WIDTH)


def _moba_sample_kernel(pt_ref, q_ref, k1_ref, v1_ref, *rest):
    del pt_ref
    n_pg = PAGES_PER_STEP
    k_pages, v_pages = rest[:n_pg], rest[n_pg:2 * n_pg]
    o_ref, wq_ref, s_ref, p_ref, gate_ref, bmax_ref, l_ref, acc_ref = rest[2 * n_pg:]
    g = pl.program_id(1)
    ks = s_ref.shape[0]
    f32 = jnp.float32
    t_new = q_ref.shape[1]
    rows = N_HEADS * t_new
    page = k_pages[0].shape[1]
    blk = MOBA_BLOCK
    pages_per_blk = blk // page
    blks_per_step = n_pg // pages_per_blk
    nblk = ks * blks_per_step
    row = lax.broadcasted_iota(jnp.int32, (rows, ATTN_WIDTH), 0)
    lane = lax.broadcasted_iota(jnp.int32, (rows, ATTN_WIDTH), 1)
    own_head = (row // t_new) == (lane // HEAD_DIM)

    @pl.when(g == 0)
    def _():
        q8 = q_ref[0].astype(f32)
        wq_ref[...] = _bf16(jnp.where(own_head, jnp.concatenate([q8] * N_HEADS, axis=0), 0.0))
        gate_ref[...] = jnp.zeros(gate_ref.shape, f32)
        bmax_ref[...] = jnp.zeros(bmax_ref.shape, f32)

    @pl.when(g < ks)
    def _():
        wq = wq_ref[...]
        lane_g = lax.broadcasted_iota(jnp.int32, (rows, LANES), 1)
        gate, bmax = gate_ref[...], bmax_ref[...]
        for b in range(blks_per_step):
            pg = range(b * pages_per_blk, (b + 1) * pages_per_blk)
            s_blk = jnp.concatenate([_dot(wq, _bf16(k_pages[p][...])) for p in pg], axis=1)
            s_ref[g, :, b * blk:(b + 1) * blk] = s_blk
            hit = lane_g == g * blks_per_step + b
            gate = jnp.where(hit, jnp.sum(s_blk, axis=-1, keepdims=True), gate)
            bmax = jnp.where(hit, jnp.max(s_blk, axis=-1, keepdims=True), bmax)
        gate_ref[...] = gate
        bmax_ref[...] = bmax

    @pl.when(g == ks - 1)
    def _():
        wq = wq_ref[...]
        blocks = [(n // blks_per_step, (n % blks_per_step) * blk) for n in range(nblk)]
        sel = _top_mask(gate_ref[...], nblk, 1)
        bias = jnp.where(sel, 0.0, -jnp.inf)
        zpad = jnp.zeros((LANES - t_new, ATTN_WIDTH), jnp.bfloat16)
        k_own = jnp.concatenate([_bf16(k1_ref[0]), zpad], axis=0)
        v_own = jnp.concatenate([_bf16(v1_ref[0]), zpad], axis=0)
        s_own = _dot_nt(wq, k_own)
        r_o = lax.broadcasted_iota(jnp.int32, s_own.shape, 0)
        c_o = lax.broadcasted_iota(jnp.int32, s_own.shape, 1)
        s_own = jnp.where(c_o <= r_o % t_new, s_own, -jnp.inf)
        m = jnp.maximum(jnp.max(jnp.where(sel, bmax_ref[...], -jnp.inf), axis=-1, keepdims=True),
                        jnp.max(s_own, axis=-1, keepdims=True))
        den = jnp.zeros((rows, blk), f32)
        for n, (gi, c0) in enumerate(blocks):
            p_blk = jnp.exp(s_ref[gi, :, c0:c0 + blk] + bias[:, n:n + 1] - m)
            den = den + p_blk
            p_ref[gi, :, c0:c0 + blk] = _bf16(p_blk)
        p_own = jnp.exp(s_own - m)
        l_ref[...] = jnp.sum(den, axis=-1, keepdims=True) + jnp.sum(p_own, axis=-1, keepdims=True)
        acc_ref[...] = _dot(_bf16(p_own), v_own)

    @pl.when(g >= ks)
    def _():
        gv = g - ks
        o = acc_ref[...]
        for p in range(n_pg):
            o = o + _dot_nt(p_ref[gv, :, p * page:(p + 1) * page], _bf16(v_pages[p][...]))
        acc_ref[...] = o

    @pl.when(g == 2 * ks - 1)
    def _():
        o64 = jnp.where(own_head, acc_ref[...] / l_ref[...], 0.0)
        o8 = o64[0:t_new]
        for h in range(1, N_HEADS):
            o8 = o8 + o64[h * t_new:(h + 1) * t_new]
        o_ref[0] = _bf16(o8)


def _moba_sample(qb, k_new, v_new, cache_k, cache_v, page_table, n_seq, t_new):
    n_phys, page = cache_k.shape[0], cache_k.shape[1]
    ck = cache_k.transpose(0, 2, 3, 1).reshape(n_phys, ATTN_WIDTH, page)
    cv = cache_v.transpose(0, 2, 3, 1).reshape(n_phys, ATTN_WIDTH, page)
    n_pages = page_table.shape[1]
    pps = PAGES_PER_STEP
    ks = n_pages // pps
    rows = N_HEADS * t_new
    assert n_pages * page // MOBA_BLOCK <= LANES
    seq_spec = pl.BlockSpec((1, t_new, ATTN_WIDTH), lambda s, g, pt: (s, 0, 0))

    def k_page_spec(p):
        return pl.BlockSpec((None, ATTN_WIDTH, page),
                            lambda s, g, pt: (pt[s, jnp.minimum(g, ks - 1) * pps + p], 0, 0))

    def v_page_spec(p):
        def index(s, g, pt):
            in_v = g >= ks
            return pt[jnp.where(in_v, s, jnp.maximum(s - 1, 0)), jnp.where(in_v, g - ks, ks - 1) * pps + p], 0, 0
        return pl.BlockSpec((None, ATTN_WIDTH, page), index)

    f32, bf = jnp.float32, jnp.bfloat16
    out = pl.pallas_call(
        _moba_sample_kernel,
        grid_spec=pltpu.PrefetchScalarGridSpec(
            num_scalar_prefetch=1,
            grid=(n_seq, 2 * ks),
            in_specs=([seq_spec, seq_spec, seq_spec] + [k_page_spec(p) for p in range(pps)]
                      + [v_page_spec(p) for p in range(pps)]),
            out_specs=seq_spec,
            scratch_shapes=[pltpu.VMEM((rows, ATTN_WIDTH), bf),
                            pltpu.VMEM((ks, rows, pps * page), f32),
                            pltpu.VMEM((ks, rows, pps * page), bf),
                            pltpu.VMEM((rows, LANES), f32),
                            pltpu.VMEM((rows, LANES), f32),
                            pltpu.VMEM((rows, 1), f32),
                            pltpu.VMEM((rows, ATTN_WIDTH), f32)]),
        out_shape=jax.ShapeDtypeStruct((n_seq, t_new, ATTN_WIDTH), bf),
        compiler_params=pltpu.CompilerParams(
            dimension_semantics=("arbitrary", "arbitrary"), vmem_limit_bytes=VMEM_LIMIT),
        name="moba_sample",
    )(page_table, qb.reshape(n_seq, t_new, ATTN_WIDTH), k_new.reshape(n_seq, t_new, ATTN_WIDTH),
      v_new.reshape(n_seq, t_new, ATTN_WIDTH), *([ck] * pps), *([cv] * pps))
    return out.reshape(n_seq * t_new, ATTN_WIDTH)


def _post_kernel(x_ref, at_ref, cg_ref, st_ref, wo_ref, g1_ref, b1_ref, wup_ref, fw_ref, fb_ref, wdn_ref,
                 g2_ref, b2_ref, y_ref, tail_ref, carry_ref, *, long_seq, alpha):
    a = at_ref.shape[1]
    rows = x_ref.shape[0]
    d_ff = wdn_ref.shape[0]
    mix = _dot(at_ref[...], wo_ref[0:a, :]) + _dot(cg_ref[...], wo_ref[a:, :])
    x1 = _layer_norm(alpha * x_ref[...] + mix, g1_ref[...], b1_ref[...])
    x1b = _bf16(x1)

    if long_seq:
        @pl.when(pl.program_id(1) == 0)
        def _():
            carry_ref[...] = st_ref[0]

    f = jnp.zeros(x1.shape, jnp.float32)
    for c0 in range(0, d_ff, FF_CHUNK):
        cols = slice(c0, c0 + FF_CHUNK)
        up_a = _dot(x1b, wup_ref[:, cols])
        up_g = _dot(x1b, wup_ref[:, d_ff + c0:d_ff + c0 + FF_CHUNK])
        a3 = up_a.reshape(rows // SUBLANES, SUBLANES, FF_CHUNK)
        if long_seq:
            prev3 = _prev_groups(a3, carry_ref[:, cols])
            carry_ref[:, cols] = a3[-1]
            tail_ref[0, :, cols] = a3[-1]
        else:
            prev3 = st_ref[:, :, cols]
            tail_ref[:, :, cols] = a3
        a_c = (_causal_conv3(a3, prev3, fw_ref, cols) + fb_ref[0:1, cols]).reshape(rows, FF_CHUNK)
        act = 0.5 * a_c * (1.0 + lax.erf(a_c * np.float32(np.sqrt(0.5))))
        f = f + _dot(_bf16(act * up_g), wdn_ref[cols, :])
    y_ref[...] = _layer_norm(alpha * x1 + f, g2_ref[...], b2_ref[...])


def _post(x2d, attn, cg, state8, w_o_b, ln1_g, ln1_b, w_up_b, fcw, fcb, w_dn_b, ln2_g, ln2_b,
          *, nb, nt, long_seq, alpha):
    rows, d = x2d.shape
    tm = rows // (nb * nt)
    d_ff = w_dn_b.shape[0]
    row_blk = lambda b, t: (b * nt + t, 0)
    if long_seq:
        st_spec = pl.BlockSpec((1, SUBLANES, d_ff), lambda b, t: (b, 0, 0))
    else:
        st_spec = pl.BlockSpec((tm // SUBLANES, SUBLANES, d_ff), lambda b, t: (t, 0, 0))
    vec = lambda v: v.reshape(1, -1)
    ln1_g, ln1_b, fcb, ln2_g, ln2_b = (vec(v) for v in (ln1_g, ln1_b, fcb, ln2_g, ln2_b))
    consts = (w_o_b, ln1_g, ln1_b, w_up_b, fcw, fcb, w_dn_b, ln2_g, ln2_b)
    return pl.pallas_call(
        functools.partial(_post_kernel, long_seq=long_seq, alpha=alpha),
        grid=(nb, nt),
        in_specs=[pl.BlockSpec((tm, d), row_blk),
                  pl.BlockSpec((tm, attn.shape[1]), row_blk),
                  pl.BlockSpec((tm, cg.shape[1]), row_blk),
                  st_spec] + [_resident(w) for w in consts],
        out_specs=[pl.BlockSpec((tm, d), row_blk), st_spec],
        out_shape=(jax.ShapeDtypeStruct((rows, d), jnp.float32),
                   jax.ShapeDtypeStruct(state8.shape, jnp.float32)),
        scratch_shapes=[pltpu.VMEM((SUBLANES, d_ff), jnp.float32)],
        compiler_params=pltpu.CompilerParams(
            dimension_semantics=("arbitrary", "arbitrary"), vmem_limit_bytes=VMEM_LIMIT),
        name="post_long" if long_seq else "post_short",
    )(x2d, attn, cg, state8, *consts)


def _pad_state(state):
    return jnp.pad(state, ((0, 0), (SUBLANES - (CONV_K - 1), 0), (0, 0)))


def kernel(x_prompt, x_sample, cache_k, cache_v, page_table, state_conv, state_ffn_conv, w_in, conv_w, w_o,
           ln1_g, ln1_b, w_up, ffn_conv_w, ffn_conv_b, w_down, ln2_g, ln2_b):
    depth = w_in.shape[0]
    bsz, seq, d_model = x_prompt.shape
    n_seq, t_new, _ = x_sample.shape
    conv_width = conv_w.shape[2]
    d_ff = w_down.shape[1]
    past_len = page_table.shape[1] * cache_k.shape[2]
    alpha = (2.0 * depth) ** 0.25
    a = ATTN_WIDTH
    assert t_new == SUBLANES and seq % ROW_TILE == 0 and (n_seq * t_new) % ROW_TILE_SHORT == 0
    assert past_len % MOBA_BLOCK == 0 and past_len // MOBA_BLOCK >= MOBA_TOPK and d_ff % FF_CHUNK == 0
    assert page_table.shape[1] % PAGES_PER_STEP == 0 and (PAGES_PER_STEP * cache_k.shape[2]) % MOBA_BLOCK == 0

    cos_p, sin_p = _rope_angles(jnp.arange(seq, dtype=jnp.int32))
    cos_s, sin_s = _rope_angles(past_len + jnp.arange(t_new, dtype=jnp.int32))
    lane_tabs_p = _rope_lane_tables(cos_p, sin_p)
    row_tabs_p = (cos_p.T, sin_p.T)
    lane_tabs_s = tuple(jnp.tile(t, (ROW_TILE_SHORT // t_new, 1)) for t in _rope_lane_tables(cos_s, sin_s))
    grid_p = dict(nb=bsz, nt=seq // ROW_TILE, long_seq=True)
    grid_s = dict(nb=1, nt=n_seq * t_new // ROW_TILE_SHORT, long_seq=False)
    zeros_conv = jnp.zeros((bsz, SUBLANES, conv_width), jnp.float32)
    zeros_ffn = jnp.zeros((bsz, SUBLANES, d_ff), jnp.float32)

    xp = x_prompt.reshape(bsz * seq, d_model)
    xs = x_sample.reshape(n_seq * t_new, d_model)
    outs = [[] for _ in range(8)]
    for l in range(depth):
        w_in_b, w_o_b, w_up_b, w_dn_b = _bf16(w_in[l]), _bf16(w_o[l]), _bf16(w_up[l]), _bf16(w_down[l])
        w_qkvt_b = w_in_b[:, 0:3 * a].T
        post_w = (w_o_b, ln1_g[l], ln1_b[l], w_up_b, ffn_conv_w[l], ffn_conv_b[l], w_dn_b, ln2_g[l], ln2_b[l])

        qt, k_rm, kt, vt, vtb, cg, tail = _proj_long(
            xp, w_in_b, w_qkvt_b, lane_tabs_p, row_tabs_p, conv_w[l], zeros_conv, bsz, seq)
        attn = _moba_prompt(qt, k_rm, vtb, bsz, seq)
        xp, ffn_tail = _post(xp, attn, cg, zeros_ffn, *post_w, alpha=alpha, **grid_p)
        new = (kt.reshape(bsz, N_HEADS, HEAD_DIM, seq).transpose(0, 3, 1, 2),
               vt.reshape(bsz, N_HEADS, HEAD_DIM, seq).transpose(0, 3, 1, 2),
               tail[:, -(CONV_K - 1):], ffn_tail[:, -(CONV_K - 1):])
        for dst, val in zip(outs[0::2], new):
            dst.append(val)

        qb, k, v, cg, tail = _proj_short(xs, w_in_b, lane_tabs_s, conv_w[l], _pad_state(state_conv[l]))
        attn = _moba_sample(qb, k, v, cache_k[l], cache_v[l], page_table, n_seq, t_new)
        xs, ffn_tail = _post(xs, attn, cg, _pad_state(state_ffn_conv[l]), *post_w, alpha=alpha, **grid_s)
        new = (k.reshape(n_seq, t_new, N_HEADS, HEAD_DIM), v.reshape(n_seq, t_new, N_HEADS, HEAD_DIM),
               tail[:, -(CONV_K - 1):], ffn_tail[:, -(CONV_K - 1):])
        for dst, val in zip(outs[1::2], new):
            dst.append(val)

    kp, ks, vp, vs, cp, cs, fp, fs = (jnp.stack(o) for o in outs)
    return (xp.reshape(bsz, seq, d_model), xs.reshape(n_seq, t_new, d_model), kp, vp, ks, vs, cp, cs, fp, fs)
```

```python
import functools

import jax
import jax.numpy as jnp
import numpy as np
from jax import lax
from jax.experimental import pallas as pl
from jax.experimental.pallas import tpu as pltpu

N_HEADS = 8
HEAD_DIM = 64
ATTN_WIDTH = N_HEADS * HEAD_DIM
ROT_DIM = HEAD_DIM // 4
ROPE_THETA = 500000.0
MOBA_BLOCK = 256
MOBA_TOPK = 3
CONV_K = 3
LN_EPS = 1e-5

LANES = 128
SUBLANES = 8
BF16_ROWS = 16
ROW_TILE = 512
ROW_TILE_SHORT = 128
FF_CHUNK = 256
PAGES_PER_STEP = 16
PAGE_SLOTS = 3
PAGE_LOOKAHEAD = 2
PAST_BLOCKS_PER_TRIP = 4
VMEM_LIMIT = 56 * 1024 * 1024
MASK_BIAS = -1e30

_NT = (((1,), (1,)), ((), ()))


def _bf16(x):
    return x.astype(jnp.bfloat16)


def _dot(a, b):
    return jnp.dot(a, b, preferred_element_type=jnp.float32)


def _dot_nt(a, b):
    return lax.dot_general(a, b, _NT, preferred_element_type=jnp.float32)


def _layer_norm(x, g, b):
    mu = jnp.mean(x, axis=-1, keepdims=True)
    xc = x - mu
    var = jnp.mean(xc * xc, axis=-1, keepdims=True)
    return xc * lax.rsqrt(var + LN_EPS) * g + b


def _resident(arr):
    return pl.BlockSpec(arr.shape, lambda *_: (0,) * arr.ndim, pipeline_mode=pl.Buffered(1))


def _causal_conv3(u3, prev3, w_ref, cols):
    sub = lax.broadcasted_iota(jnp.int32, u3.shape, 1)
    s1 = jnp.where(sub < 1, pltpu.roll(prev3, 1, 1), pltpu.roll(u3, 1, 1))
    s2 = jnp.where(sub < 2, pltpu.roll(prev3, 2, 1), pltpu.roll(u3, 2, 1))
    return s2 * w_ref[0:1, cols] + s1 * w_ref[1:2, cols] + u3 * w_ref[2:3, cols]


def _prev_groups(u3, carry):
    return jnp.concatenate([carry[None], u3[:-1]], axis=0)


def _rope_lanes(z, ra_ref, rb_ref, rc_ref):
    outs = []
    for c in range(z.shape[1] // LANES):
        zc = z[:, c * LANES:(c + 1) * LANES]
        outs.append(zc * ra_ref[...]
                    + pltpu.roll(zc, LANES - ROT_DIM // 2, 1) * rb_ref[...]
                    + pltpu.roll(zc, ROT_DIM // 2, 1) * rc_ref[...])
    return jnp.concatenate(outs, axis=1)


def _rope_rows(zt, cos_ref, sin_ref):
    half = ROT_DIM // 2
    cos, sin = cos_ref[...], sin_ref[...]
    parts = []
    for h in range(N_HEADS):
        r0 = h * HEAD_DIM
        x1, x2 = zt[r0:r0 + half], zt[r0 + half:r0 + ROT_DIM]
        parts += [x1 * cos - x2 * sin, x2 * cos + x1 * sin, zt[r0 + ROT_DIM:r0 + HEAD_DIM]]
    return jnp.concatenate(parts, axis=0)


def _gated_conv(xb, w_ref, cw_ref, prev3_fn, cg_ref):
    a = ATTN_WIDTH
    c = cw_ref.shape[1]
    rows = xb.shape[0]
    gb = _dot(xb, w_ref[:, 3 * a:3 * a + c])
    u = _dot(xb, w_ref[:, 3 * a + c:3 * a + 2 * c]) * _dot(xb, w_ref[:, 3 * a + 2 * c:3 * a + 3 * c])
    u3 = u.reshape(rows // SUBLANES, SUBLANES, c)
    y = _causal_conv3(u3, prev3_fn(u3), cw_ref, slice(None)).reshape(rows, c)
    cg_ref[...] = _bf16(gb * y)
    return u3


def _proj_long_kernel(x_ref, w_ref, wt_ref, ra_ref, rb_ref, rc_ref, cos_ref, sin_ref, cw_ref, st_ref,
                      qt_ref, k_ref, kt_ref, vt_ref, vtb_ref, cg_ref, tail_ref, carry_ref):
    a = ATTN_WIDTH
    xb = _bf16(x_ref[...])
    qkvt = _dot_nt(wt_ref[...], xb)
    qt_ref[0] = _bf16(_rope_rows(qkvt[0:a], cos_ref, sin_ref) * (HEAD_DIM ** -0.5))
    kt_ref[0] = _rope_rows(qkvt[a:2 * a], cos_ref, sin_ref)
    vt = qkvt[2 * a:3 * a]
    vt_ref[0] = vt
    for c in range(vtb_ref.shape[1]):
        vtb_ref[0, c] = _bf16(vt[:, c * MOBA_BLOCK:(c + 1) * MOBA_BLOCK])
    k_ref[...] = _bf16(_rope_lanes(_dot(xb, w_ref[:, a:2 * a]), ra_ref, rb_ref, rc_ref))

    @pl.when(pl.program_id(1) == 0)
    def _():
        carry_ref[...] = st_ref[0]

    u3 = _gated_conv(xb, w_ref, cw_ref, lambda u3: _prev_groups(u3, carry_ref[...]), cg_ref)
    carry_ref[...] = u3[-1]
    tail_ref[0] = u3[-1]


def _proj_short_kernel(x_ref, w_ref, ra_ref, rb_ref, rc_ref, cw_ref, st_ref,
                       q_ref, k_ref, v_ref, cg_ref, tail_ref):
    a = ATTN_WIDTH
    xb = _bf16(x_ref[...])
    q = _rope_lanes(_dot(xb, w_ref[:, 0:a]), ra_ref, rb_ref, rc_ref)
    q_ref[...] = _bf16(q * (HEAD_DIM ** -0.5))
    k_ref[...] = _rope_lanes(_dot(xb, w_ref[:, a:2 * a]), ra_ref, rb_ref, rc_ref)
    v_ref[...] = _dot(xb, w_ref[:, 2 * a:3 * a])
    tail_ref[...] = _gated_conv(xb, w_ref, cw_ref, lambda u3: st_ref[...], cg_ref)


def _proj_long(x2d, w_in_b, w_qkvt_b, lane_tabs, row_tabs, conv_w, state8, bsz, seq):
    rows, d = x2d.shape
    tm = ROW_TILE
    nt = seq // tm
    a, c = ATTN_WIDTH, conv_w.shape[1]
    nblk = tm // MOBA_BLOCK
    row_blk = lambda b, t: (b * nt + t, 0)
    tab_spec = pl.BlockSpec((tm, LANES), lambda b, t: (t, 0))
    rtab_spec = pl.BlockSpec((ROT_DIM // 2, tm), lambda b, t: (0, t))
    st_spec = pl.BlockSpec((1, SUBLANES, c), lambda b, t: (b, 0, 0))
    kt_spec = pl.BlockSpec((1, a, tm), lambda b, t: (b, 0, t))
    vtb_spec = pl.BlockSpec((1, nblk, a, MOBA_BLOCK), lambda b, t: (b, t, 0, 0))
    f32, bf = jnp.float32, jnp.bfloat16
    out_shape = (
        jax.ShapeDtypeStruct((bsz, a, seq), bf),
        jax.ShapeDtypeStruct((rows, a), bf),
        jax.ShapeDtypeStruct((bsz, a, seq), f32), jax.ShapeDtypeStruct((bsz, a, seq), f32),
        jax.ShapeDtypeStruct((bsz, seq // MOBA_BLOCK, a, MOBA_BLOCK), bf),
        jax.ShapeDtypeStruct((rows, c), bf),
        jax.ShapeDtypeStruct(state8.shape, f32),
    )
    return pl.pallas_call(
        _proj_long_kernel,
        grid=(bsz, nt),
        in_specs=[pl.BlockSpec((tm, d), row_blk), _resident(w_in_b), _resident(w_qkvt_b),
                  tab_spec, tab_spec, tab_spec, rtab_spec, rtab_spec, _resident(conv_w), st_spec],
        out_specs=[kt_spec, pl.BlockSpec((tm, a), row_blk), kt_spec, kt_spec, vtb_spec,
                   pl.BlockSpec((tm, c), row_blk), st_spec],
        out_shape=out_shape,
        scratch_shapes=[pltpu.VMEM((SUBLANES, c), f32)],
        compiler_params=pltpu.CompilerParams(
            dimension_semantics=("arbitrary", "arbitrary"), vmem_limit_bytes=VMEM_LIMIT),
        name="proj_long",
    )(x2d, w_in_b, w_qkvt_b, *lane_tabs, *row_tabs, conv_w, state8)


def _proj_short(x2d, w_in_b, lane_tabs, conv_w, state8):
    rows, d = x2d.shape
    tm = ROW_TILE_SHORT
    a, c = ATTN_WIDTH, conv_w.shape[1]
    row_blk = lambda t: (t, 0)
    tab_spec = pl.BlockSpec((tm, LANES), lambda t: (0, 0))
    st_spec = pl.BlockSpec((tm // SUBLANES, SUBLANES, c), lambda t: (t, 0, 0))
    f32, bf = jnp.float32, jnp.bfloat16
    out_shape = (
        jax.ShapeDtypeStruct((rows, a), bf), jax.ShapeDtypeStruct((rows, a), f32),
        jax.ShapeDtypeStruct((rows, a), f32), jax.ShapeDtypeStruct((rows, c), bf),
        jax.ShapeDtypeStruct(state8.shape, f32),
    )
    return pl.pallas_call(
        _proj_short_kernel,
        grid=(rows // tm,),
        in_specs=[pl.BlockSpec((tm, d), row_blk), _resident(w_in_b),
                  tab_spec, tab_spec, tab_spec, _resident(conv_w), st_spec],
        out_specs=[pl.BlockSpec((tm, a), row_blk)] * 3 + [pl.BlockSpec((tm, c), row_blk), st_spec],
        out_shape=out_shape,
        compiler_params=pltpu.CompilerParams(
            dimension_semantics=("arbitrary",), vmem_limit_bytes=VMEM_LIMIT),
        name="proj_short",
    )(x2d, w_in_b, *lane_tabs, conv_w, state8)


def _rope_angles(pos):
    half = ROT_DIM // 2
    inv = 1.0 / (ROPE_THETA ** (jnp.arange(half, dtype=jnp.float32) / half))
    ang = pos.astype(jnp.float32)[:, None] * inv[None, :]
    return jnp.cos(ang), jnp.sin(ang)


def _rope_lane_tables(cos, sin):
    t, half = cos.shape
    rest = HEAD_DIM - ROT_DIM
    ones, zeros = jnp.ones((t, rest), jnp.float32), jnp.zeros((t, rest), jnp.float32)
    zh = jnp.zeros((t, half), jnp.float32)
    ta = jnp.concatenate([cos, cos, ones], axis=1)
    tb = jnp.concatenate([-sin, zh, zeros], axis=1)
    tc = jnp.concatenate([zh, sin, zeros], axis=1)
    return tuple(jnp.tile(x, (1, LANES // HEAD_DIM)) for x in (ta, tb, tc))


def _top_mask(gate, n_valid, axis):
    idx = lax.broadcasted_iota(jnp.int32, gate.shape, axis)
    n = gate.shape[axis]
    g = jnp.where(idx < n_valid, gate, -jnp.inf)
    sel = jnp.zeros(gate.shape, jnp.bool_)
    for _ in range(MOBA_TOPK):
        mx = jnp.max(g, axis=axis, keepdims=True)
        first = jnp.min(jnp.where(g == mx, idx, n), axis=axis, keepdims=True)
        pick = (idx == first) & (mx > -jnp.inf)
        sel = sel | pick
        g = jnp.where(pick, -jnp.inf, g)
    return sel


def _moba_prompt_kernel(qt_ref, k_ref, vt_ref, o_ref, km_ref, m_ref, acc_ref):
    i = pl.program_id(2)
    blk = MOBA_BLOCK
    nb = vt_ref.shape[1]
    nbp = -(-nb // BF16_ROWS) * BF16_ROWS
    f32, bf = jnp.float32, jnp.bfloat16

    @pl.when(i == 0)
    def _():
        kf = k_ref[0].astype(f32).reshape(nb, blk, LANES)
        km_ref[...] = jnp.zeros(km_ref.shape, f32)
        km_ref[0:nb, :] = jnp.sum(kf, axis=1) * (1.0 / blk)

    row_q = lax.broadcasted_iota(jnp.int32, (LANES, blk), 0)
    head0 = _bf16(jnp.where(row_q < HEAD_DIM, 1.0, 0.0))
    head1 = _bf16(jnp.where(row_q < HEAD_DIM, 0.0, 1.0))
    qt = qt_ref[0]
    q2 = jnp.concatenate([qt * head0, qt * head1], axis=1)
    gate = _dot(km_ref[...], q2.astype(f32))[0:nbp]
    row_g = lax.broadcasted_iota(jnp.int32, gate.shape, 0)
    visible = _top_mask(gate, i, 0) | (row_g == i)
    bias = _bf16(jnp.where(visible, 0.0, MASK_BIAS))
    q_aug = jnp.concatenate([q2, bias, jnp.zeros((LANES - nbp, 2 * blk), bf)], axis=0)

    lane_k = lax.broadcasted_iota(jnp.int32, (blk, LANES), 1)
    ones_rows = jnp.ones((BF16_ROWS, blk), bf)

    def block(j):
        start = pl.multiple_of(j * blk, blk)
        k_aug = jnp.concatenate([k_ref[0, pl.ds(start, blk), :],
                                 _bf16(jnp.where(lane_k == j, 1.0, 0.0))], axis=1)
        st = _dot(k_aug, q_aug)
        vv = jnp.concatenate([vt_ref[0, j], ones_rows], axis=0)
        return st, vv

    def fold(sts, vvs, first):
        m_new = functools.reduce(jnp.maximum, [jnp.max(st, axis=0, keepdims=True) for st in sts])
        if not first:
            m_new = jnp.maximum(m_ref[...], m_new)
        pv = functools.reduce(jnp.add, [_dot(vv, _bf16(jnp.exp(st - m_new))) for st, vv in zip(sts, vvs)])
        if first:
            acc_ref[...] = pv
        else:
            acc_ref[...] = acc_ref[...] * jnp.exp(m_ref[...] - m_new) + pv
        m_ref[...] = m_new

    st, vv = block(i)
    key = lax.broadcasted_iota(jnp.int32, st.shape, 0)
    col = lax.broadcasted_iota(jnp.int32, st.shape, 1)
    qpos = jnp.where(col >= blk, col - blk, col)
    fold([jnp.where(key <= qpos, st, -jnp.inf)], [vv], True)

    def fold_blocks(js):
        sts, vvs = zip(*[block(j) for j in js])
        fold(sts, vvs, False)

    per_trip = PAST_BLOCKS_PER_TRIP

    def body(t, carry):
        fold_blocks([per_trip * t + u for u in range(per_trip)])
        return carry

    lax.fori_loop(0, i // per_trip, body, 0)
    done = i // per_trip * per_trip
    width = per_trip // 2
    while width:
        @pl.when(((i - done) & width) != 0)
        def _(width=width):
            first = done + ((i - done) & ~(2 * width - 1))
            fold_blocks([first + u for u in range(width)])
        width //= 2

    acc = acc_ref[...]
    ot = acc[0:LANES] / acc[LANES:LANES + 1]
    x = jnp.concatenate([ot[0:HEAD_DIM, 0:blk], ot[HEAD_DIM:LANES, blk:2 * blk]], axis=0)
    o_ref[0] = _bf16(x.T)


def _moba_prompt(qt, k_rm, vtb, bsz, seq):
    blk = MOBA_BLOCK
    nb = seq // blk
    assert nb <= LANES
    k3 = k_rm.reshape(bsz, seq, ATTN_WIDTH)
    qt_spec = pl.BlockSpec((1, LANES, blk), lambda b, h, i: (b, h, i))
    k_spec = pl.BlockSpec((1, seq, LANES), lambda b, h, i: (b, 0, h))
    vt_spec = pl.BlockSpec((1, nb, LANES, blk), lambda b, h, i: (b, 0, h, 0))
    out = pl.pallas_call(
        _moba_prompt_kernel,
        grid=(bsz, ATTN_WIDTH // LANES, nb),
        in_specs=[qt_spec, k_spec, vt_spec],
        out_specs=pl.BlockSpec((1, blk, LANES), lambda b, h, i: (b, i, h)),
        out_shape=jax.ShapeDtypeStruct((bsz, seq, ATTN_WIDTH), jnp.bfloat16),
        scratch_shapes=[pltpu.VMEM((LANES, LANES), jnp.float32),
                        pltpu.VMEM((1, 2 * blk), jnp.float32),
                        pltpu.VMEM((LANES + BF16_ROWS, 2 * blk), jnp.float32)],
        compiler_params=pltpu.CompilerParams(
            dimension_semantics=("arbitrary", "arbitrary", "arbitrary"), vmem_limit_bytes=VMEM_LIMIT),
        name="moba_prompt",
    )(qt, k3, vtb)
    return out.reshape(bsz * seq, ATTN_WIDTH)


def _moba_sample_kernel(pt_ref, q_ref, k1_ref, v1_ref, ck_ref, cv_ref, o_ref,
                        buf_ref, sem_ref, wq_ref, s_ref, p_ref, gate_ref, bmax_ref, l_ref, acc_ref):
    seq, g = pl.program_id(0), pl.program_id(1)
    steps = pl.num_programs(1)
    ks = s_ref.shape[0]
    n_slot, n_pg = buf_ref.shape[0], buf_ref.shape[1]
    step = seq * steps + g
    total_steps = pl.num_programs(0) * steps

    def page_copy(src_ref, page_id, slot, p):
        return pltpu.make_async_copy(src_ref.at[page_id], buf_ref.at[slot, p], sem_ref.at[slot])

    def fetch(ahead):
        g_raw = g + ahead
        wrap = (g_raw >= steps).astype(jnp.int32)
        seq_f, g_f = seq + wrap, g_raw - wrap * steps
        slot = (step + ahead) % n_slot

        @pl.when(g_f < ks)
        def _():
            for p in range(n_pg):
                page_copy(ck_ref, pt_ref[seq_f, g_f * n_pg + p], slot, p).start()

        @pl.when(g_f >= ks)
        def _():
            for p in range(n_pg):
                page_copy(cv_ref, pt_ref[seq_f, (g_f - ks) * n_pg + p], slot, p).start()

    @pl.when(step == 0)
    def _():
        for ahead in range(PAGE_LOOKAHEAD):
            fetch(ahead)

    @pl.when(step + PAGE_LOOKAHEAD < total_steps)
    def _():
        fetch(PAGE_LOOKAHEAD)

    slot = step % n_slot
    for p in range(n_pg):
        page_copy(ck_ref, 0, slot, p).wait()
    k_pages = v_pages = [buf_ref.at[slot, p] for p in range(n_pg)]

    f32 = jnp.float32
    t_new = q_ref.shape[1]
    rows = N_HEADS * t_new
    page = buf_ref.shape[3]
    blk = MOBA_BLOCK
    pages_per_blk = blk // page
    blks_per_step = n_pg // pages_per_blk
    nblk = ks * blks_per_step
    row = lax.broadcasted_iota(jnp.int32, (rows, ATTN_WIDTH), 0)
    lane = lax.broadcasted_iota(jnp.int32, (rows, ATTN_WIDTH), 1)
    own_head = (row // t_new) == (lane // HEAD_DIM)

    @pl.when(g == 0)
    def _():
        q8 = q_ref[0].astype(f32)
        wq_ref[...] = _bf16(jnp.where(own_head, jnp.concatenate([q8] * N_HEADS, axis=0), 0.0))
        gate_ref[...] = jnp.zeros(gate_ref.shape, f32)
        bmax_ref[...] = jnp.zeros(bmax_ref.shape, f32)

    @pl.when(g < ks)
    def _():
        wq = wq_ref[...]
        lane_g = lax.broadcasted_iota(jnp.int32, (rows, LANES), 1)
        gate, bmax = gate_ref[...], bmax_ref[...]
        for b in range(blks_per_step):
            pg = range(b * pages_per_blk, (b + 1) * pages_per_blk)
            s_blk = jnp.concatenate([_dot(wq, _bf16(k_pages[p][...])) for p in pg], axis=1)
            s_ref[g, :, b * blk:(b + 1) * blk] = s_blk
            hit = lane_g == g * blks_per_step + b
            gate = jnp.where(hit, jnp.sum(s_blk, axis=-1, keepdims=True), gate)
            bmax = jnp.where(hit, jnp.max(s_blk, axis=-1, keepdims=True), bmax)
        gate_ref[...] = gate
        bmax_ref[...] = bmax

    @pl.when(g == ks - 1)
    def _():
        wq = wq_ref[...]
        blocks = [(n // blks_per_step, (n % blks_per_step) * blk) for n in range(nblk)]
        sel = _top_mask(gate_ref[...], nblk, 1)
        bias = jnp.where(sel, 0.0, -jnp.inf)
        zpad = jnp.zeros((LANES - t_new, ATTN_WIDTH), jnp.bfloat16)
        k_own = jnp.concatenate([_bf16(k1_ref[0]), zpad], axis=0)
        v_own = jnp.concatenate([_bf16(v1_ref[0]), zpad], axis=0)
        s_own = _dot_nt(wq, k_own)
        r_o = lax.broadcasted_iota(jnp.int32, s_own.shape, 0)
        c_o = lax.broadcasted_iota(jnp.int32, s_own.shape, 1)
        s_own = jnp.where(c_o <= r_o % t_new, s_own, -jnp.inf)
        m = jnp.maximum(jnp.max(jnp.where(sel, bmax_ref[...], -jnp.inf), axis=-1, keepdims=True),
                        jnp.max(s_own, axis=-1, keepdims=True))
        den = jnp.zeros((rows, blk), f32)
        for n, (gi, c0) in enumerate(blocks):
            p_blk = jnp.exp(s_ref[gi, :, c0:c0 + blk] + bias[:, n:n + 1] - m)
            den = den + p_blk
            p_ref[gi, :, c0:c0 + blk] = _bf16(p_blk)
        p_own = jnp.exp(s_own - m)
        l_ref[...] = jnp.sum(den, axis=-1, keepdims=True) + jnp.sum(p_own, axis=-1, keepdims=True)
        acc_ref[...] = _dot(_bf16(p_own), v_own)

    @pl.when(g >= ks)
    def _():
        gv = g - ks
        o = acc_ref[...]
        for p in range(n_pg):
            o = o + _dot_nt(p_ref[gv, :, p * page:(p + 1) * page], _bf16(v_pages[p][...]))
        acc_ref[...] = o

    @pl.when(g == 2 * ks - 1)
    def _():
        o64 = jnp.where(own_head, acc_ref[...] / l_ref[...], 0.0)
        o8 = o64[0:t_new]
        for h in range(1, N_HEADS):
            o8 = o8 + o64[h * t_new:(h + 1) * t_new]
        o_ref[0] = _bf16(o8)


def _moba_sample(qb, k_new, v_new, cache_k, cache_v, page_table, n_seq, t_new):
    n_phys, page = cache_k.shape[0], cache_k.shape[1]
    ck = cache_k.transpose(0, 2, 3, 1).reshape(n_phys, ATTN_WIDTH, page)
    cv = cache_v.transpose(0, 2, 3, 1).reshape(n_phys, ATTN_WIDTH, page)
    n_pages = page_table.shape[1]
    pps = PAGES_PER_STEP
    ks = n_pages // pps
    rows = N_HEADS * t_new
    assert n_pages * page // MOBA_BLOCK <= LANES
    assert PAGE_LOOKAHEAD < PAGE_SLOTS and PAGE_LOOKAHEAD <= 2 * ks
    seq_spec = pl.BlockSpec((1, t_new, ATTN_WIDTH), lambda s, g, pt: (s, 0, 0))
    hbm_spec = pl.BlockSpec(memory_space=pl.ANY)

    f32, bf = jnp.float32, jnp.bfloat16
    out = pl.pallas_call(
        _moba_sample_kernel,
        grid_spec=pltpu.PrefetchScalarGridSpec(
            num_scalar_prefetch=1,
            grid=(n_seq, 2 * ks),
            in_specs=[seq_spec, seq_spec, seq_spec, hbm_spec, hbm_spec],
            out_specs=seq_spec,
            scratch_shapes=[pltpu.VMEM((PAGE_SLOTS, pps, ATTN_WIDTH, page), f32),
                            pltpu.SemaphoreType.DMA((PAGE_SLOTS,)),
                            pltpu.VMEM((rows, ATTN_WIDTH), bf),
                            pltpu.VMEM((ks, rows, pps * page), f32),
                            pltpu.VMEM((ks, rows, pps * page), bf),
                            pltpu.VMEM((rows, LANES), f32),
                            pltpu.VMEM((rows, LANES), f32),
                            pltpu.VMEM((rows, 1), f32),
                            pltpu.VMEM((rows, ATTN_WIDTH), f32)]),
        out_shape=jax.ShapeDtypeStruct((n_seq, t_new, ATTN_WIDTH), bf),
        compiler_params=pltpu.CompilerParams(
            dimension_semantics=("arbitrary", "arbitrary"), vmem_limit_bytes=VMEM_LIMIT),
        name="moba_sample",
    )(page_table, qb.reshape(n_seq, t_new, ATTN_WIDTH), k_new.reshape(n_seq, t_new, ATTN_WIDTH),
      v_new.reshape(n_seq, t_new, ATTN_WIDTH), ck, cv)
    return out.reshape(n_seq * t_new, ATTN_WIDTH)


def _post_kernel(x_ref, at_ref, cg_ref, st_ref, wo_ref, g1_ref, b1_ref, wup_ref, fw_ref, fb_ref, wdn_ref,
                 g2_ref, b2_ref, y_ref, tail_ref, carry_ref, *, long_seq, alpha):
    a = at_ref.shape[1]
    rows = x_ref.shape[0]
    d_ff = wdn_ref.shape[0]
    mix = _dot(at_ref[...], wo_ref[0:a, :]) + _dot(cg_ref[...], wo_ref[a:, :])
    x1 = _layer_norm(alpha * x_ref[...] + mix, g1_ref[...], b1_ref[...])
    x1b = _bf16(x1)

    if long_seq:
        @pl.when(pl.program_id(1) == 0)
        def _():
            carry_ref[...] = st_ref[0]

    f = jnp.zeros(x1.shape, jnp.float32)
    for c0 in range(0, d_ff, FF_CHUNK):
        cols = slice(c0, c0 + FF_CHUNK)
        up_a = _dot(x1b, wup_ref[:, cols])
        up_g = _dot(x1b, wup_ref[:, d_ff + c0:d_ff + c0 + FF_CHUNK])
        a3 = up_a.reshape(rows // SUBLANES, SUBLANES, FF_CHUNK)
        if long_seq:
            prev3 = _prev_groups(a3, carry_ref[:, cols])
            carry_ref[:, cols] = a3[-1]
            tail_ref[0, :, cols] = a3[-1]
        else:
            prev3 = st_ref[:, :, cols]
            tail_ref[:, :, cols] = a3
        a_c = (_causal_conv3(a3, prev3, fw_ref, cols) + fb_ref[0:1, cols]).reshape(rows, FF_CHUNK)
        act = 0.5 * a_c * (1.0 + lax.erf(a_c * np.float32(np.sqrt(0.5))))
        f = f + _dot(_bf16(act * up_g), wdn_ref[cols, :])
    y_ref[...] = _layer_norm(alpha * x1 + f, g2_ref[...], b2_ref[...])


def _post(x2d, attn, cg, state8, w_o_b, ln1_g, ln1_b, w_up_b, fcw, fcb, w_dn_b, ln2_g, ln2_b,
          *, nb, nt, long_seq, alpha):
    rows, d = x2d.shape
    tm = rows // (nb * nt)
    d_ff = w_dn_b.shape[0]
    row_blk = lambda b, t: (b * nt + t, 0)
    if long_seq:
        st_spec = pl.BlockSpec((1, SUBLANES, d_ff), lambda b, t: (b, 0, 0))
    else:
        st_spec = pl.BlockSpec((tm // SUBLANES, SUBLANES, d_ff), lambda b, t: (t, 0, 0))
    vec = lambda v: v.reshape(1, -1)
    ln1_g, ln1_b, fcb, ln2_g, ln2_b = (vec(v) for v in (ln1_g, ln1_b, fcb, ln2_g, ln2_b))
    consts = (w_o_b, ln1_g, ln1_b, w_up_b, fcw, fcb, w_dn_b, ln2_g, ln2_b)
    return pl.pallas_call(
        functools.partial(_post_kernel, long_seq=long_seq, alpha=alpha),
        grid=(nb, nt),
        in_specs=[pl.BlockSpec((tm, d), row_blk),
                  pl.BlockSpec((tm, attn.shape[1]), row_blk),
                  pl.BlockSpec((tm, cg.shape[1]), row_blk),
                  st_spec] + [_resident(w) for w in consts],
        out_specs=[pl.BlockSpec((tm, d), row_blk), st_spec],
        out_shape=(jax.ShapeDtypeStruct((rows, d), jnp.float32),
                   jax.ShapeDtypeStruct(state8.shape, jnp.float32)),
        scratch_shapes=[pltpu.VMEM((SUBLANES, d_ff), jnp.float32)],
        compiler_params=pltpu.CompilerParams(
            dimension_semantics=("arbitrary", "arbitrary"), vmem_limit_bytes=VMEM_LIMIT),
        name="post_long" if long_seq else "post_short",
    )(x2d, attn, cg, state8, *consts)


def _pad_state(state):
    return jnp.pad(state, ((0, 0), (SUBLANES - (CONV_K - 1), 0), (0, 0)))


def kernel(x_prompt, x_sample, cache_k, cache_v, page_table, state_conv, state_ffn_conv, w_in, conv_w, w_o,
           ln1_g, ln1_b, w_up, ffn_conv_w, ffn_conv_b, w_down, ln2_g, ln2_b):
    depth = w_in.shape[0]
    bsz, seq, d_model = x_prompt.shape
    n_seq, t_new, _ = x_sample.shape
    conv_width = conv_w.shape[2]
    d_ff = w_down.shape[1]
    past_len = page_table.shape[1] * cache_k.shape[2]
    alpha = (2.0 * depth) ** 0.25
    a = ATTN_WIDTH
    assert t_new == SUBLANES and seq % ROW_TILE == 0 and (n_seq * t_new) % ROW_TILE_SHORT == 0
    assert past_len % MOBA_BLOCK == 0 and past_len // MOBA_BLOCK >= MOBA_TOPK and d_ff % FF_CHUNK == 0
    assert page_table.shape[1] % PAGES_PER_STEP == 0 and (PAGES_PER_STEP * cache_k.shape[2]) % MOBA_BLOCK == 0

    cos_p, sin_p = _rope_angles(jnp.arange(seq, dtype=jnp.int32))
    cos_s, sin_s = _rope_angles(past_len + jnp.arange(t_new, dtype=jnp.int32))
    lane_tabs_p = _rope_lane_tables(cos_p, sin_p)
    row_tabs_p = (cos_p.T, sin_p.T)
    lane_tabs_s = tuple(jnp.tile(t, (ROW_TILE_SHORT // t_new, 1)) for t in _rope_lane_tables(cos_s, sin_s))
    grid_p = dict(nb=bsz, nt=seq // ROW_TILE, long_seq=True)
    grid_s = dict(nb=1, nt=n_seq * t_new // ROW_TILE_SHORT, long_seq=False)
    zeros_conv = jnp.zeros((bsz, SUBLANES, conv_width), jnp.float32)
    zeros_ffn = jnp.zeros((bsz, SUBLANES, d_ff), jnp.float32)

    xp = x_prompt.reshape(bsz * seq, d_model)
    xs = x_sample.reshape(n_seq * t_new, d_model)
    outs = [[] for _ in range(8)]
    for l in range(depth):
        w_in_b, w_o_b, w_up_b, w_dn_b = _bf16(w_in[l]), _bf16(w_o[l]), _bf16(w_up[l]), _bf16(w_down[l])
        w_qkvt_b = w_in_b[:, 0:3 * a].T
        post_w = (w_o_b, ln1_g[l], ln1_b[l], w_up_b, ffn_conv_w[l], ffn_conv_b[l], w_dn_b, ln2_g[l], ln2_b[l])

        qt, k_rm, kt, vt, vtb, cg, tail = _proj_long(
            xp, w_in_b, w_qkvt_b, lane_tabs_p, row_tabs_p, conv_w[l], zeros_conv, bsz, seq)
        attn = _moba_prompt(qt, k_rm, vtb, bsz, seq)
        xp, ffn_tail = _post(xp, attn, cg, zeros_ffn, *post_w, alpha=alpha, **grid_p)
        new = (kt.reshape(bsz, N_HEADS, HEAD_DIM, seq).transpose(0, 3, 1, 2),
               vt.reshape(bsz, N_HEADS, HEAD_DIM, seq).transpose(0, 3, 1, 2),
               tail[:, -(CONV_K - 1):], ffn_tail[:, -(CONV_K - 1):])
        for dst, val in zip(outs[0::2], new):
            dst.append(val)

        qb, k, v, cg, tail = _proj_short(xs, w_in_b, lane_tabs_s, conv_w[l], _pad_state(state_conv[l]))
        attn = _moba_sample(qb, k, v, cache_k[l], cache_v[l], page_table, n_seq, t_new)
        xs, ffn_tail = _post(xs, attn, cg, _pad_state(state_ffn_conv[l]), *post_w, alpha=alpha, **grid_s)
        new = (k.reshape(n_seq, t_new, N_HEADS, HEAD_DIM), v.reshape(n_seq, t_new, N_HEADS, HEAD_DIM),
               tail[:, -(CONV_K - 1):], ffn_tail[:, -(CONV_K - 1):])
        for dst, val in zip(outs[1::2], new):
            dst.append(val)

    kp, ks, vp, vs, cp, cs, fp, fs = (jnp.stack(o) for o in outs)
    return (xp.reshape(bsz, seq, d_model), xs.reshape(n_seq, t_new, d_model), kp, vp, ks, vs, cp, cs, fp, fs)
```

```python
import functools

import jax
import jax.numpy as jnp
import numpy as np
from jax import lax
from jax.experimental import pallas as pl
from jax.experimental.pallas import tpu as pltpu

N_HEADS = 8
HEAD_DIM = 64
ATTN_WIDTH = N_HEADS * HEAD_DIM
ROT_DIM = HEAD_DIM // 4
ROPE_THETA = 500000.0
MOBA_BLOCK = 256
MOBA_TOPK = 3
CONV_K = 3
LN_EPS = 1e-5

LANES = 128
SUBLANES = 8
BF16_ROWS = 16
ROW_TILE = 512
ROW_TILE_SHORT = 256
FF_CHUNK = 256
PAGES_PER_STEP = 16
PAGE_SLOTS = 3
PAGE_LOOKAHEAD = 2
PAST_BLOCKS_PER_TRIP = 4
VMEM_LIMIT = 56 * 1024 * 1024
MASK_BIAS = -1e30

_NT = (((1,), (1,)), ((), ()))


def _bf16(x):
    return x.astype(jnp.bfloat16)


def _dot(a, b):
    return jnp.dot(a, b, preferred_element_type=jnp.float32)


def _dot_nt(a, b):
    return lax.dot_general(a, b, _NT, preferred_element_type=jnp.float32)


def _layer_norm(x, g, b):
    mu = jnp.mean(x, axis=-1, keepdims=True)
    xc = x - mu
    var = jnp.mean(xc * xc, axis=-1, keepdims=True)
    return xc * lax.rsqrt(var + LN_EPS) * g + b


def _resident(arr):
    return pl.BlockSpec(arr.shape, lambda *_: (0,) * arr.ndim, pipeline_mode=pl.Buffered(1))


def _causal_conv3(u3, prev3, w_ref, cols):
    sub = lax.broadcasted_iota(jnp.int32, u3.shape, 1)
    s1 = jnp.where(sub < 1, pltpu.roll(prev3, 1, 1), pltpu.roll(u3, 1, 1))
    s2 = jnp.where(sub < 2, pltpu.roll(prev3, 2, 1), pltpu.roll(u3, 2, 1))
    return s2 * w_ref[0:1, cols] + s1 * w_ref[1:2, cols] + u3 * w_ref[2:3, cols]


def _prev_groups(u3, carry):
    return jnp.concatenate([carry[None], u3[:-1]], axis=0)


def _rope_lanes(z, ra_ref, rb_ref, rc_ref):
    outs = []
    for c in range(z.shape[1] // LANES):
        zc = z[:, c * LANES:(c + 1) * LANES]
        outs.append(zc * ra_ref[...]
                    + pltpu.roll(zc, LANES - ROT_DIM // 2, 1) * rb_ref[...]
                    + pltpu.roll(zc, ROT_DIM // 2, 1) * rc_ref[...])
    return jnp.concatenate(outs, axis=1)


def _rope_rows(zt, cos_ref, sin_ref):
    half = ROT_DIM // 2
    cos, sin = cos_ref[...], sin_ref[...]
    parts = []
    for h in range(N_HEADS):
        r0 = h * HEAD_DIM
        x1, x2 = zt[r0:r0 + half], zt[r0 + half:r0 + ROT_DIM]
        parts += [x1 * cos - x2 * sin, x2 * cos + x1 * sin, zt[r0 + ROT_DIM:r0 + HEAD_DIM]]
    return jnp.concatenate(parts, axis=0)


def _gated_conv(xb, w_ref, cw_ref, prev3_fn, cg_ref):
    a = ATTN_WIDTH
    c = cw_ref.shape[1]
    rows = xb.shape[0]
    gb = _dot(xb, w_ref[:, 3 * a:3 * a + c])
    u = _dot(xb, w_ref[:, 3 * a + c:3 * a + 2 * c]) * _dot(xb, w_ref[:, 3 * a + 2 * c:3 * a + 3 * c])
    u3 = u.reshape(rows // SUBLANES, SUBLANES, c)
    y = _causal_conv3(u3, prev3_fn(u3), cw_ref, slice(None)).reshape(rows, c)
    cg_ref[...] = _bf16(gb * y)
    return u3


def _proj_long_kernel(x_ref, w_ref, wt_ref, ra_ref, rb_ref, rc_ref, cos_ref, sin_ref, cw_ref, st_ref,
                      qt_ref, k_ref, kt_ref, vt_ref, vtb_ref, cg_ref, tail_ref, carry_ref):
    a = ATTN_WIDTH
    xb = _bf16(x_ref[...])
    qkvt = _dot_nt(wt_ref[...], xb)
    qt_ref[0] = _bf16(_rope_rows(qkvt[0:a], cos_ref, sin_ref) * (HEAD_DIM ** -0.5))
    kt_ref[0] = _rope_rows(qkvt[a:2 * a], cos_ref, sin_ref)
    vt = qkvt[2 * a:3 * a]
    vt_ref[0] = vt
    for c in range(vtb_ref.shape[1]):
        vtb_ref[0, c] = _bf16(vt[:, c * MOBA_BLOCK:(c + 1) * MOBA_BLOCK])
    k_ref[...] = _bf16(_rope_lanes(_dot(xb, w_ref[:, a:2 * a]), ra_ref, rb_ref, rc_ref))

    @pl.when(pl.program_id(1) == 0)
    def _():
        carry_ref[...] = st_ref[0]

    u3 = _gated_conv(xb, w_ref, cw_ref, lambda u3: _prev_groups(u3, carry_ref[...]), cg_ref)
    carry_ref[...] = u3[-1]
    tail_ref[0] = u3[-1]


def _proj_short_kernel(x_ref, w_ref, ra_ref, rb_ref, rc_ref, cw_ref, st_ref,
                       q_ref, k_ref, v_ref, cg_ref, tail_ref):
    a = ATTN_WIDTH
    xb = _bf16(x_ref[...])
    q = _rope_lanes(_dot(xb, w_ref[:, 0:a]), ra_ref, rb_ref, rc_ref)
    q_ref[...] = _bf16(q * (HEAD_DIM ** -0.5))
    k_ref[...] = _rope_lanes(_dot(xb, w_ref[:, a:2 * a]), ra_ref, rb_ref, rc_ref)
    v_ref[...] = _dot(xb, w_ref[:, 2 * a:3 * a])
    tail_ref[...] = _gated_conv(xb, w_ref, cw_ref, lambda u3: st_ref[...], cg_ref)


def _proj_long(x2d, w_in_b, w_qkvt_b, lane_tabs, row_tabs, conv_w, state8, bsz, seq):
    rows, d = x2d.shape
    tm = ROW_TILE
    nt = seq // tm
    a, c = ATTN_WIDTH, conv_w.shape[1]
    nblk = tm // MOBA_BLOCK
    row_blk = lambda b, t: (b * nt + t, 0)
    tab_spec = pl.BlockSpec((tm, LANES), lambda b, t: (t, 0))
    rtab_spec = pl.BlockSpec((ROT_DIM // 2, tm), lambda b, t: (0, t))
    st_spec = pl.BlockSpec((1, SUBLANES, c), lambda b, t: (b, 0, 0))
    kt_spec = pl.BlockSpec((1, a, tm), lambda b, t: (b, 0, t))
    vtb_spec = pl.BlockSpec((1, nblk, a, MOBA_BLOCK), lambda b, t: (b, t, 0, 0))
    f32, bf = jnp.float32, jnp.bfloat16
    out_shape = (
        jax.ShapeDtypeStruct((bsz, a, seq), bf),
        jax.ShapeDtypeStruct((rows, a), bf),
        jax.ShapeDtypeStruct((bsz, a, seq), f32), jax.ShapeDtypeStruct((bsz, a, seq), f32),
        jax.ShapeDtypeStruct((bsz, seq // MOBA_BLOCK, a, MOBA_BLOCK), bf),
        jax.ShapeDtypeStruct((rows, c), bf),
        jax.ShapeDtypeStruct(state8.shape, f32),
    )
    return pl.pallas_call(
        _proj_long_kernel,
        grid=(bsz, nt),
        in_specs=[pl.BlockSpec((tm, d), row_blk), _resident(w_in_b), _resident(w_qkvt_b),
                  tab_spec, tab_spec, tab_spec, rtab_spec, rtab_spec, _resident(conv_w), st_spec],
        out_specs=[kt_spec, pl.BlockSpec((tm, a), row_blk), kt_spec, kt_spec, vtb_spec,
                   pl.BlockSpec((tm, c), row_blk), st_spec],
        out_shape=out_shape,
        scratch_shapes=[pltpu.VMEM((SUBLANES, c), f32)],
        compiler_params=pltpu.CompilerParams(
            dimension_semantics=("arbitrary", "arbitrary"), vmem_limit_bytes=VMEM_LIMIT),
        name="proj_long",
    )(x2d, w_in_b, w_qkvt_b, *lane_tabs, *row_tabs, conv_w, state8)


def _proj_short(x2d, w_in_b, lane_tabs, conv_w, state8):
    rows, d = x2d.shape
    tm = ROW_TILE_SHORT
    a, c = ATTN_WIDTH, conv_w.shape[1]
    row_blk = lambda t: (t, 0)
    tab_spec = pl.BlockSpec((tm, LANES), lambda t: (0, 0))
    st_spec = pl.BlockSpec((tm // SUBLANES, SUBLANES, c), lambda t: (t, 0, 0))
    f32, bf = jnp.float32, jnp.bfloat16
    out_shape = (
        jax.ShapeDtypeStruct((rows, a), bf), jax.ShapeDtypeStruct((rows, a), f32),
        jax.ShapeDtypeStruct((rows, a), f32), jax.ShapeDtypeStruct((rows, c), bf),
        jax.ShapeDtypeStruct(state8.shape, f32),
    )
    return pl.pallas_call(
        _proj_short_kernel,
        grid=(rows // tm,),
        in_specs=[pl.BlockSpec((tm, d), row_blk), _resident(w_in_b),
                  tab_spec, tab_spec, tab_spec, _resident(conv_w), st_spec],
        out_specs=[pl.BlockSpec((tm, a), row_blk)] * 3 + [pl.BlockSpec((tm, c), row_blk), st_spec],
        out_shape=out_shape,
        compiler_params=pltpu.CompilerParams(
            dimension_semantics=("arbitrary",), vmem_limit_bytes=VMEM_LIMIT),
        name="proj_short",
    )(x2d, w_in_b, *lane_tabs, conv_w, state8)


def _rope_angles(pos):
    half = ROT_DIM // 2
    inv = 1.0 / (ROPE_THETA ** (jnp.arange(half, dtype=jnp.float32) / half))
    ang = pos.astype(jnp.float32)[:, None] * inv[None, :]
    return jnp.cos(ang), jnp.sin(ang)


def _rope_lane_tables(cos, sin):
    t, half = cos.shape
    rest = HEAD_DIM - ROT_DIM
    ones, zeros = jnp.ones((t, rest), jnp.float32), jnp.zeros((t, rest), jnp.float32)
    zh = jnp.zeros((t, half), jnp.float32)
    ta = jnp.concatenate([cos, cos, ones], axis=1)
    tb = jnp.concatenate([-sin, zh, zeros], axis=1)
    tc = jnp.concatenate([zh, sin, zeros], axis=1)
    return tuple(jnp.tile(x, (1, LANES // HEAD_DIM)) for x in (ta, tb, tc))


def _top_mask(gate, n_valid, axis):
    idx = lax.broadcasted_iota(jnp.int32, gate.shape, axis)
    n = gate.shape[axis]
    g = jnp.where(idx < n_valid, gate, -jnp.inf)
    sel = jnp.zeros(gate.shape, jnp.bool_)
    for _ in range(MOBA_TOPK):
        mx = jnp.max(g, axis=axis, keepdims=True)
        first = jnp.min(jnp.where(g == mx, idx, n), axis=axis, keepdims=True)
        pick = (idx == first) & (mx > -jnp.inf)
        sel = sel | pick
        g = jnp.where(pick, -jnp.inf, g)
    return sel


def _moba_prompt_kernel(qt_ref, k_ref, vt_ref, o_ref, km_ref, m_ref, acc_ref):
    i = pl.program_id(2)
    blk = MOBA_BLOCK
    nb = vt_ref.shape[1]
    nbp = -(-nb // BF16_ROWS) * BF16_ROWS
    f32, bf = jnp.float32, jnp.bfloat16

    @pl.when(i == 0)
    def _():
        kf = k_ref[0].astype(f32).reshape(nb, blk, LANES)
        km_ref[...] = jnp.zeros(km_ref.shape, f32)
        km_ref[0:nb, :] = jnp.sum(kf, axis=1) * (1.0 / blk)

    row_q = lax.broadcasted_iota(jnp.int32, (LANES, blk), 0)
    head0 = _bf16(jnp.where(row_q < HEAD_DIM, 1.0, 0.0))
    head1 = _bf16(jnp.where(row_q < HEAD_DIM, 0.0, 1.0))
    qt = qt_ref[0]
    q2 = jnp.concatenate([qt * head0, qt * head1], axis=1)
    gate = _dot(km_ref[...], q2.astype(f32))[0:nbp]
    row_g = lax.broadcasted_iota(jnp.int32, gate.shape, 0)
    visible = _top_mask(gate, i, 0) | (row_g == i)
    bias = _bf16(jnp.where(visible, 0.0, MASK_BIAS))
    q_aug = jnp.concatenate([q2, bias, jnp.zeros((LANES - nbp, 2 * blk), bf)], axis=0)

    lane_k = lax.broadcasted_iota(jnp.int32, (blk, LANES), 1)
    ones_rows = jnp.ones((BF16_ROWS, blk), bf)

    def block(j):
        start = pl.multiple_of(j * blk, blk)
        k_aug = jnp.concatenate([k_ref[0, pl.ds(start, blk), :],
                                 _bf16(jnp.where(lane_k == j, 1.0, 0.0))], axis=1)
        st = _dot(k_aug, q_aug)
        vv = jnp.concatenate([vt_ref[0, j], ones_rows], axis=0)
        return st, vv

    def fold(state, sts, vvs):
        m_new = functools.reduce(jnp.maximum, [jnp.max(st, axis=0, keepdims=True) for st in sts])
        if state is not None:
            m_new = jnp.maximum(state[0], m_new)
        pv = functools.reduce(jnp.add, [_dot(vv, _bf16(jnp.exp(st - m_new))) for st, vv in zip(sts, vvs)])
        if state is not None:
            pv = state[1] * jnp.exp(state[0] - m_new) + pv
        return m_new, pv

    per_trip = PAST_BLOCKS_PER_TRIP
    n_tail = i % per_trip
    for k in range(per_trip):
        @pl.when(n_tail == k)
        def _(k=k):
            st, vv = block(i)
            key = lax.broadcasted_iota(jnp.int32, st.shape, 0)
            col = lax.broadcasted_iota(jnp.int32, st.shape, 1)
            qpos = jnp.where(col >= blk, col - blk, col)
            sts, vvs = [jnp.where(key <= qpos, st, -jnp.inf)], [vv]
            for u in range(k):
                st, vv = block(i - k + u)
                sts.append(st)
                vvs.append(vv)
            m_ref[...], acc_ref[...] = fold(None, sts, vvs)

    def body(t, carry):
        sts, vvs = zip(*[block(per_trip * t + u) for u in range(per_trip)])
        m_ref[...], acc_ref[...] = fold((m_ref[...], acc_ref[...]), sts, vvs)
        return carry

    lax.fori_loop(0, i // per_trip, body, 0)

    acc = acc_ref[...]
    ot = acc[0:LANES] / acc[LANES:LANES + 1]
    x = jnp.concatenate([ot[0:HEAD_DIM, 0:blk], ot[HEAD_DIM:LANES, blk:2 * blk]], axis=0)
    o_ref[0] = _bf16(x.T)


def _moba_prompt(qt, k_rm, vtb, bsz, seq):
    blk = MOBA_BLOCK
    nb = seq // blk
    assert nb <= LANES
    k3 = k_rm.reshape(bsz, seq, ATTN_WIDTH)
    qt_spec = pl.BlockSpec((1, LANES, blk), lambda b, h, i: (b, h, i))
    k_spec = pl.BlockSpec((1, seq, LANES), lambda b, h, i: (b, 0, h))
    vt_spec = pl.BlockSpec((1, nb, LANES, blk), lambda b, h, i: (b, 0, h, 0))
    out = pl.pallas_call(
        _moba_prompt_kernel,
        grid=(bsz, ATTN_WIDTH // LANES, nb),
        in_specs=[qt_spec, k_spec, vt_spec],
        out_specs=pl.BlockSpec((1, blk, LANES), lambda b, h, i: (b, i, h)),
        out_shape=jax.ShapeDtypeStruct((bsz, seq, ATTN_WIDTH), jnp.bfloat16),
        scratch_shapes=[pltpu.VMEM((LANES, LANES), jnp.float32),
                        pltpu.VMEM((1, 2 * blk), jnp.float32),
                        pltpu.VMEM((LANES + BF16_ROWS, 2 * blk), jnp.float32)],
        compiler_params=pltpu.CompilerParams(
            dimension_semantics=("arbitrary", "arbitrary", "arbitrary"), vmem_limit_bytes=VMEM_LIMIT),
        name="moba_prompt",
    )(qt, k3, vtb)
    return out.reshape(bsz * seq, ATTN_WIDTH)


def _moba_sample_kernel(pt_ref, q_ref, k1_ref, v1_ref, ck_ref, cv_ref, o_ref,
                        buf_ref, sem_ref, wq_ref, s_ref, bias_ref, gate_ref, bmax_ref, m_ref, l_ref, acc_ref):
    seq, g = pl.program_id(0), pl.program_id(1)
    steps = pl.num_programs(1)
    ks = s_ref.shape[0]
    n_slot, n_pg = buf_ref.shape[0], buf_ref.shape[1]
    step = seq * steps + g
    total_steps = pl.num_programs(0) * steps

    def page_copy(src_ref, page_id, slot, p):
        return pltpu.make_async_copy(src_ref.at[page_id], buf_ref.at[slot, p], sem_ref.at[slot])

    def fetch(ahead):
        g_raw = g + ahead
        wrap = (g_raw >= steps).astype(jnp.int32)
        seq_f, g_f = seq + wrap, g_raw - wrap * steps
        slot = (step + ahead) % n_slot

        @pl.when(g_f < ks)
        def _():
            for p in range(n_pg):
                page_copy(ck_ref, pt_ref[seq_f, g_f * n_pg + p], slot, p).start()

        @pl.when(g_f >= ks)
        def _():
            for p in range(n_pg):
                page_copy(cv_ref, pt_ref[seq_f, (g_f - ks) * n_pg + p], slot, p).start()

    @pl.when(step == 0)
    def _():
        for ahead in range(PAGE_LOOKAHEAD):
            fetch(ahead)

    @pl.when(step + PAGE_LOOKAHEAD < total_steps)
    def _():
        fetch(PAGE_LOOKAHEAD)

    slot = step % n_slot
    for p in range(n_pg):
        page_copy(ck_ref, 0, slot, p).wait()
    k_pages = v_pages = [buf_ref.at[slot, p] for p in range(n_pg)]

    f32 = jnp.float32
    t_new = q_ref.shape[1]
    rows = N_HEADS * t_new
    page = buf_ref.shape[3]
    blk = MOBA_BLOCK
    pages_per_blk = blk // page
    blks_per_step = n_pg // pages_per_blk
    nblk = ks * blks_per_step
    row = lax.broadcasted_iota(jnp.int32, (rows, ATTN_WIDTH), 0)
    lane = lax.broadcasted_iota(jnp.int32, (rows, ATTN_WIDTH), 1)
    own_head = (row // t_new) == (lane // HEAD_DIM)

    @pl.when(g == 0)
    def _():
        q8 = q_ref[0].astype(f32)
        wq_ref[...] = _bf16(jnp.where(own_head, jnp.concatenate([q8] * N_HEADS, axis=0), 0.0))
        gate_ref[...] = jnp.zeros(gate_ref.shape, f32)
        bmax_ref[...] = jnp.zeros(bmax_ref.shape, f32)

    @pl.when(g < ks)
    def _():
        wq = wq_ref[...]
        lane_g = lax.broadcasted_iota(jnp.int32, (rows, LANES), 1)
        gate, bmax = gate_ref[...], bmax_ref[...]
        for b in range(blks_per_step):
            pg = range(b * pages_per_blk, (b + 1) * pages_per_blk)
            s_blk = jnp.concatenate([_dot(wq, _bf16(k_pages[p][...])) for p in pg], axis=1)
            s_ref[g, :, b * blk:(b + 1) * blk] = s_blk
            hit = lane_g == g * blks_per_step + b
            gate = jnp.where(hit, jnp.sum(s_blk, axis=-1, keepdims=True), gate)
            bmax = jnp.where(hit, jnp.max(s_blk, axis=-1, keepdims=True), bmax)
        gate_ref[...] = gate
        bmax_ref[...] = bmax

    @pl.when(g == ks - 1)
    def _():
        wq = wq_ref[...]
        sel = _top_mask(gate_ref[...], nblk, 1)
        bias = jnp.where(sel, 0.0, -jnp.inf)
        for gi in range(ks):
            shift = (LANES - gi * blks_per_step) % LANES
            bias_ref[gi] = pltpu.roll(bias, shift, 1) if shift else bias
        zpad = jnp.zeros((LANES - t_new, ATTN_WIDTH), jnp.bfloat16)
        k_own = jnp.concatenate([_bf16(k1_ref[0]), zpad], axis=0)
        v_own = jnp.concatenate([_bf16(v1_ref[0]), zpad], axis=0)
        s_own = _dot_nt(wq, k_own)
        r_o = lax.broadcasted_iota(jnp.int32, s_own.shape, 0)
        c_o = lax.broadcasted_iota(jnp.int32, s_own.shape, 1)
        s_own = jnp.where(c_o <= r_o % t_new, s_own, -jnp.inf)
        m = jnp.maximum(jnp.max(jnp.where(sel, bmax_ref[...], -jnp.inf), axis=-1, keepdims=True),
                        jnp.max(s_own, axis=-1, keepdims=True))
        m_ref[...] = m
        p_own = jnp.exp(s_own - m)
        l_ref[...] = jnp.sum(p_own, axis=-1, keepdims=True)
        acc_ref[...] = _dot(_bf16(p_own), v_own)

    @pl.when(g >= ks)
    def _():
        gv = g - ks
        m, bias = m_ref[...], bias_ref[gv]
        den = jnp.zeros((rows, blk), f32)
        o = acc_ref[...]
        for b in range(blks_per_step):
            p_blk = jnp.exp(s_ref[gv, :, b * blk:(b + 1) * blk] + bias[:, b:b + 1] - m)
            den = den + p_blk
            pb = _bf16(p_blk)
            for x in range(pages_per_blk):
                o = o + _dot_nt(pb[:, x * page:(x + 1) * page], _bf16(v_pages[b * pages_per_blk + x][...]))
        acc_ref[...] = o
        l_ref[...] = l_ref[...] + jnp.sum(den, axis=-1, keepdims=True)

    @pl.when(g == 2 * ks - 1)
    def _():
        o64 = jnp.where(own_head, acc_ref[...] / l_ref[...], 0.0)
        o8 = o64[0:t_new]
        for h in range(1, N_HEADS):
            o8 = o8 + o64[h * t_new:(h + 1) * t_new]
        o_ref[0] = _bf16(o8)


def _moba_sample(qb, k_new, v_new, cache_k, cache_v, page_table, n_seq, t_new):
    n_phys, page = cache_k.shape[0], cache_k.shape[1]
    ck = cache_k.transpose(0, 2, 3, 1).reshape(n_phys, ATTN_WIDTH, page)
    cv = cache_v.transpose(0, 2, 3, 1).reshape(n_phys, ATTN_WIDTH, page)
    n_pages = page_table.shape[1]
    pps = PAGES_PER_STEP
    ks = n_pages // pps
    rows = N_HEADS * t_new
    assert n_pages * page // MOBA_BLOCK <= LANES
    assert PAGE_LOOKAHEAD < PAGE_SLOTS and PAGE_LOOKAHEAD <= 2 * ks
    seq_spec = pl.BlockSpec((1, t_new, ATTN_WIDTH), lambda s, g, pt: (s, 0, 0))
    hbm_spec = pl.BlockSpec(memory_space=pl.ANY)

    f32, bf = jnp.float32, jnp.bfloat16
    out = pl.pallas_call(
        _moba_sample_kernel,
        grid_spec=pltpu.PrefetchScalarGridSpec(
            num_scalar_prefetch=1,
            grid=(n_seq, 2 * ks),
            in_specs=[seq_spec, seq_spec, seq_spec, hbm_spec, hbm_spec],
            out_specs=seq_spec,
            scratch_shapes=[pltpu.VMEM((PAGE_SLOTS, pps, ATTN_WIDTH, page), f32),
                            pltpu.SemaphoreType.DMA((PAGE_SLOTS,)),
                            pltpu.VMEM((rows, ATTN_WIDTH), bf),
                            pltpu.VMEM((ks, rows, pps * page), f32),
                            pltpu.VMEM((ks, rows, LANES), f32),
                            pltpu.VMEM((rows, LANES), f32),
                            pltpu.VMEM((rows, LANES), f32),
                            pltpu.VMEM((rows, 1), f32),
                            pltpu.VMEM((rows, 1), f32),
                            pltpu.VMEM((rows, ATTN_WIDTH), f32)]),
        out_shape=jax.ShapeDtypeStruct((n_seq, t_new, ATTN_WIDTH), bf),
        compiler_params=pltpu.CompilerParams(
            dimension_semantics=("arbitrary", "arbitrary"), vmem_limit_bytes=VMEM_LIMIT),
        name="moba_sample",
    )(page_table, qb.reshape(n_seq, t_new, ATTN_WIDTH), k_new.reshape(n_seq, t_new, ATTN_WIDTH),
      v_new.reshape(n_seq, t_new, ATTN_WIDTH), ck, cv)
    return out.reshape(n_seq * t_new, ATTN_WIDTH)


def _post_kernel(x_ref, at_ref, cg_ref, st_ref, wo_ref, g1_ref, b1_ref, wup_ref, fw_ref, fb_ref, wdn_ref,
                 g2_ref, b2_ref, y_ref, tail_ref, carry_ref, *, long_seq, alpha):
    a = at_ref.shape[1]
    rows = x_ref.shape[0]
    d_ff = wdn_ref.shape[0]
    mix = _dot(at_ref[...], wo_ref[0:a, :]) + _dot(cg_ref[...], wo_ref[a:, :])
    x1 = _layer_norm(alpha * x_ref[...] + mix, g1_ref[...], b1_ref[...])
    x1b = _bf16(x1)

    if long_seq:
        @pl.when(pl.program_id(1) == 0)
        def _():
            carry_ref[...] = st_ref[0]

    f = jnp.zeros(x1.shape, jnp.float32)
    for c0 in range(0, d_ff, FF_CHUNK):
        cols = slice(c0, c0 + FF_CHUNK)
        up_a = _dot(x1b, wup_ref[:, cols])
        up_g = _dot(x1b, wup_ref[:, d_ff + c0:d_ff + c0 + FF_CHUNK])
        a3 = up_a.reshape(rows // SUBLANES, SUBLANES, FF_CHUNK)
        if long_seq:
            prev3 = _prev_groups(a3, carry_ref[:, cols])
            carry_ref[:, cols] = a3[-1]
            tail_ref[0, :, cols] = a3[-1]
        else:
            prev3 = st_ref[:, :, cols]
            tail_ref[:, :, cols] = a3
        a_c = (_causal_conv3(a3, prev3, fw_ref, cols) + fb_ref[0:1, cols]).reshape(rows, FF_CHUNK)
        act = 0.5 * a_c * (1.0 + lax.erf(a_c * np.float32(np.sqrt(0.5))))
        f = f + _dot(_bf16(act * up_g), wdn_ref[cols, :])
    y_ref[...] = _layer_norm(alpha * x1 + f, g2_ref[...], b2_ref[...])


def _post(x2d, attn, cg, state8, w_o_b, ln1_g, ln1_b, w_up_b, fcw, fcb, w_dn_b, ln2_g, ln2_b,
          *, nb, nt, long_seq, alpha):
    rows, d = x2d.shape
    tm = rows // (nb * nt)
    d_ff = w_dn_b.shape[0]
    row_blk = lambda b, t: (b * nt + t, 0)
    if long_seq:
        st_spec = pl.BlockSpec((1, SUBLANES, d_ff), lambda b, t: (b, 0, 0))
    else:
        st_spec = pl.BlockSpec((tm // SUBLANES, SUBLANES, d_ff), lambda b, t: (t, 0, 0))
    vec = lambda v: v.reshape(1, -1)
    ln1_g, ln1_b, fcb, ln2_g, ln2_b = (vec(v) for v in (ln1_g, ln1_b, fcb, ln2_g, ln2_b))
    consts = (w_o_b, ln1_g, ln1_b, w_up_b, fcw, fcb, w_dn_b, ln2_g, ln2_b)
    return pl.pallas_call(
        functools.partial(_post_kernel, long_seq=long_seq, alpha=alpha),
        grid=(nb, nt),
        in_specs=[pl.BlockSpec((tm, d), row_blk),
                  pl.BlockSpec((tm, attn.shape[1]), row_blk),
                  pl.BlockSpec((tm, cg.shape[1]), row_blk),
                  st_spec] + [_resident(w) for w in consts],
        out_specs=[pl.BlockSpec((tm, d), row_blk), st_spec],
        out_shape=(jax.ShapeDtypeStruct((rows, d), jnp.float32),
                   jax.ShapeDtypeStruct(state8.shape, jnp.float32)),
        scratch_shapes=[pltpu.VMEM((SUBLANES, d_ff), jnp.float32)],
        compiler_params=pltpu.CompilerParams(
            dimension_semantics=("arbitrary", "arbitrary"), vmem_limit_bytes=VMEM_LIMIT),
        name="post_long" if long_seq else "post_short",
    )(x2d, attn, cg, state8, *consts)


def _pad_state(state):
    return jnp.pad(state, ((0, 0), (SUBLANES - (CONV_K - 1), 0), (0, 0)))


def kernel(x_prompt, x_sample, cache_k, cache_v, page_table, state_conv, state_ffn_conv, w_in, conv_w, w_o,
           ln1_g, ln1_b, w_up, ffn_conv_w, ffn_conv_b, w_down, ln2_g, ln2_b):
    depth = w_in.shape[0]
    bsz, seq, d_model = x_prompt.shape
    n_seq, t_new, _ = x_sample.shape
    conv_width = conv_w.shape[2]
    d_ff = w_down.shape[1]
    past_len = page_table.shape[1] * cache_k.shape[2]
    alpha = (2.0 * depth) ** 0.25
    a = ATTN_WIDTH
    assert t_new == SUBLANES and seq % ROW_TILE == 0 and (n_seq * t_new) % ROW_TILE_SHORT == 0
    assert past_len % MOBA_BLOCK == 0 and past_len // MOBA_BLOCK >= MOBA_TOPK and d_ff % FF_CHUNK == 0
    assert page_table.shape[1] % PAGES_PER_STEP == 0 and (PAGES_PER_STEP * cache_k.shape[2]) % MOBA_BLOCK == 0

    cos_p, sin_p = _rope_angles(jnp.arange(seq, dtype=jnp.int32))
    cos_s, sin_s = _rope_angles(past_len + jnp.arange(t_new, dtype=jnp.int32))
    lane_tabs_p = _rope_lane_tables(cos_p, sin_p)
    row_tabs_p = (cos_p.T, sin_p.T)
    lane_tabs_s = tuple(jnp.tile(t, (ROW_TILE_SHORT // t_new, 1)) for t in _rope_lane_tables(cos_s, sin_s))
    grid_p = dict(nb=bsz, nt=seq // ROW_TILE, long_seq=True)
    grid_s = dict(nb=1, nt=n_seq * t_new // ROW_TILE_SHORT, long_seq=False)
    zeros_conv = jnp.zeros((bsz, SUBLANES, conv_width), jnp.float32)
    zeros_ffn = jnp.zeros((bsz, SUBLANES, d_ff), jnp.float32)

    xp = x_prompt.reshape(bsz * seq, d_model)
    xs = x_sample.reshape(n_seq * t_new, d_model)
    outs = [[] for _ in range(8)]
    for l in range(depth):
        w_in_b, w_o_b, w_up_b, w_dn_b = _bf16(w_in[l]), _bf16(w_o[l]), _bf16(w_up[l]), _bf16(w_down[l])
        w_qkvt_b = w_in_b[:, 0:3 * a].T
        post_w = (w_o_b, ln1_g[l], ln1_b[l], w_up_b, ffn_conv_w[l], ffn_conv_b[l], w_dn_b, ln2_g[l], ln2_b[l])

        qt, k_rm, kt, vt, vtb, cg, tail = _proj_long(
            xp, w_in_b, w_qkvt_b, lane_tabs_p, row_tabs_p, conv_w[l], zeros_conv, bsz, seq)
        attn = _moba_prompt(qt, k_rm, vtb, bsz, seq)
        xp, ffn_tail = _post(xp, attn, cg, zeros_ffn, *post_w, alpha=alpha, **grid_p)
        new = (kt.reshape(bsz, N_HEADS, HEAD_DIM, seq).transpose(0, 3, 1, 2),
               vt.reshape(bsz, N_HEADS, HEAD_DIM, seq).transpose(0, 3, 1, 2),
               tail[:, -(CONV_K - 1):], ffn_tail[:, -(CONV_K - 1):])
        for dst, val in zip(outs[0::2], new):
            dst.append(val)

        qb, k, v, cg, tail = _proj_short(xs, w_in_b, lane_tabs_s, conv_w[l], _pad_state(state_conv[l]))
        attn = _moba_sample(qb, k, v, cache_k[l], cache_v[l], page_table, n_seq, t_new)
        xs, ffn_tail = _post(xs, attn, cg, _pad_state(state_ffn_conv[l]), *post_w, alpha=alpha, **grid_s)
        new = (k.reshape(n_seq, t_new, N_HEADS, HEAD_DIM), v.reshape(n_seq, t_new, N_HEADS, HEAD_DIM),
               tail[:, -(CONV_K - 1):], ffn_tail[:, -(CONV_K - 1):])
        for dst, val in zip(outs[1::2], new):
            dst.append(val)

    kp, ks, vp, vs, cp, cs, fp, fs = (jnp.stack(o) for o in outs)
    return (xp.reshape(bsz, seq, d_model), xs.reshape(n_seq, t_new, d_model), kp, vp, ks, vs, cp, cs, fp, fs)
```

```python
import functools

import jax
import jax.numpy as jnp
import numpy as np
from jax import lax
from jax.experimental import pallas as pl
from jax.experimental.pallas import tpu as pltpu

N_HEADS = 8
HEAD_DIM = 64
ATTN_WIDTH = N_HEADS * HEAD_DIM
ROT_DIM = HEAD_DIM // 4
ROPE_THETA = 500000.0
MOBA_BLOCK = 256
MOBA_TOPK = 3
CONV_K = 3
LN_EPS = 1e-5

LANES = 128
SUBLANES = 8
BF16_ROWS = 16
ROW_TILE = 512
ROW_TILE_SHORT = 256
FF_CHUNK = 256
PAGES_PER_STEP = 32
PAGE_SLOTS = 3
PAGE_LOOKAHEAD = 2
PAST_BLOCKS_PER_TRIP = 4
VMEM_LIMIT = 56 * 1024 * 1024
MASK_BIAS = -1e30

_NT = (((1,), (1,)), ((), ()))


def _bf16(x):
    return x.astype(jnp.bfloat16)


def _dot(a, b):
    return jnp.dot(a, b, preferred_element_type=jnp.float32)


def _dot_nt(a, b):
    return lax.dot_general(a, b, _NT, preferred_element_type=jnp.float32)


def _layer_norm(x, g, b):
    mu = jnp.mean(x, axis=-1, keepdims=True)
    xc = x - mu
    var = jnp.mean(xc * xc, axis=-1, keepdims=True)
    return xc * lax.rsqrt(var + LN_EPS) * g + b


def _resident(arr):
    return pl.BlockSpec(arr.shape, lambda *_: (0,) * arr.ndim, pipeline_mode=pl.Buffered(1))


def _causal_conv3(u3, prev3, w_ref, cols):
    sub = lax.broadcasted_iota(jnp.int32, u3.shape, 1)
    s1 = jnp.where(sub < 1, pltpu.roll(prev3, 1, 1), pltpu.roll(u3, 1, 1))
    s2 = jnp.where(sub < 2, pltpu.roll(prev3, 2, 1), pltpu.roll(u3, 2, 1))
    return s2 * w_ref[0:1, cols] + s1 * w_ref[1:2, cols] + u3 * w_ref[2:3, cols]


def _prev_groups(u3, carry):
    return jnp.concatenate([carry[None], u3[:-1]], axis=0)


def _rope_lanes(z, ra_ref, rb_ref, rc_ref):
    outs = []
    for c in range(z.shape[1] // LANES):
        zc = z[:, c * LANES:(c + 1) * LANES]
        outs.append(zc * ra_ref[...]
                    + pltpu.roll(zc, LANES - ROT_DIM // 2, 1) * rb_ref[...]
                    + pltpu.roll(zc, ROT_DIM // 2, 1) * rc_ref[...])
    return jnp.concatenate(outs, axis=1)


def _rope_rows(zt, cos_ref, sin_ref):
    half = ROT_DIM // 2
    cos, sin = cos_ref[...], sin_ref[...]
    parts = []
    for h in range(N_HEADS):
        r0 = h * HEAD_DIM
        x1, x2 = zt[r0:r0 + half], zt[r0 + half:r0 + ROT_DIM]
        parts += [x1 * cos - x2 * sin, x2 * cos + x1 * sin, zt[r0 + ROT_DIM:r0 + HEAD_DIM]]
    return jnp.concatenate(parts, axis=0)


def _gated_conv(xb, w_ref, cw_ref, prev3_fn, cg_ref):
    a = ATTN_WIDTH
    c = cw_ref.shape[1]
    rows = xb.shape[0]
    gb = _dot(xb, w_ref[:, 3 * a:3 * a + c])
    u = _dot(xb, w_ref[:, 3 * a + c:3 * a + 2 * c]) * _dot(xb, w_ref[:, 3 * a + 2 * c:3 * a + 3 * c])
    u3 = u.reshape(rows // SUBLANES, SUBLANES, c)
    y = _causal_conv3(u3, prev3_fn(u3), cw_ref, slice(None)).reshape(rows, c)
    cg_ref[...] = _bf16(gb * y)
    return u3


def _proj_long_kernel(x_ref, w_ref, wt_ref, ra_ref, rb_ref, rc_ref, cos_ref, sin_ref, cw_ref, st_ref,
                      qt_ref, k_ref, kt_ref, vt_ref, vtb_ref, cg_ref, tail_ref, carry_ref):
    a = ATTN_WIDTH
    xb = _bf16(x_ref[...])
    qkvt = _dot_nt(wt_ref[...], xb)
    qt_ref[0] = _bf16(_rope_rows(qkvt[0:a], cos_ref, sin_ref) * (HEAD_DIM ** -0.5))
    kt_ref[0] = _rope_rows(qkvt[a:2 * a], cos_ref, sin_ref)
    vt = qkvt[2 * a:3 * a]
    vt_ref[0] = vt
    for c in range(vtb_ref.shape[1]):
        vtb_ref[0, c] = _bf16(vt[:, c * MOBA_BLOCK:(c + 1) * MOBA_BLOCK])
    k_ref[...] = _bf16(_rope_lanes(_dot(xb, w_ref[:, a:2 * a]), ra_ref, rb_ref, rc_ref))

    @pl.when(pl.program_id(1) == 0)
    def _():
        carry_ref[...] = st_ref[0]

    u3 = _gated_conv(xb, w_ref, cw_ref, lambda u3: _prev_groups(u3, carry_ref[...]), cg_ref)
    carry_ref[...] = u3[-1]
    tail_ref[0] = u3[-1]


def _proj_short_kernel(x_ref, w_ref, ra_ref, rb_ref, rc_ref, cw_ref, st_ref,
                       q_ref, k_ref, v_ref, cg_ref, tail_ref):
    a = ATTN_WIDTH
    xb = _bf16(x_ref[...])
    q = _rope_lanes(_dot(xb, w_ref[:, 0:a]), ra_ref, rb_ref, rc_ref)
    q_ref[...] = _bf16(q * (HEAD_DIM ** -0.5))
    k_ref[...] = _rope_lanes(_dot(xb, w_ref[:, a:2 * a]), ra_ref, rb_ref, rc_ref)
    v_ref[...] = _dot(xb, w_ref[:, 2 * a:3 * a])
    tail_ref[...] = _gated_conv(xb, w_ref, cw_ref, lambda u3: st_ref[...], cg_ref)


def _proj_long(x2d, w_in_b, w_qkvt_b, lane_tabs, row_tabs, conv_w, state8, bsz, seq):
    rows, d = x2d.shape
    tm = ROW_TILE
    nt = seq // tm
    a, c = ATTN_WIDTH, conv_w.shape[1]
    nblk = tm // MOBA_BLOCK
    row_blk = lambda b, t: (b * nt + t, 0)
    tab_spec = pl.BlockSpec((tm, LANES), lambda b, t: (t, 0))
    rtab_spec = pl.BlockSpec((ROT_DIM // 2, tm), lambda b, t: (0, t))
    st_spec = pl.BlockSpec((1, SUBLANES, c), lambda b, t: (b, 0, 0))
    kt_spec = pl.BlockSpec((1, a, tm), lambda b, t: (b, 0, t))
    vtb_spec = pl.BlockSpec((1, nblk, a, MOBA_BLOCK), lambda b, t: (b, t, 0, 0))
    f32, bf = jnp.float32, jnp.bfloat16
    out_shape = (
        jax.ShapeDtypeStruct((bsz, a, seq), bf),
        jax.ShapeDtypeStruct((rows, a), bf),
        jax.ShapeDtypeStruct((bsz, a, seq), f32), jax.ShapeDtypeStruct((bsz, a, seq), f32),
        jax.ShapeDtypeStruct((bsz, seq // MOBA_BLOCK, a, MOBA_BLOCK), bf),
        jax.ShapeDtypeStruct((rows, c), bf),
        jax.ShapeDtypeStruct(state8.shape, f32),
    )
    return pl.pallas_call(
        _proj_long_kernel,
        grid=(bsz, nt),
        in_specs=[pl.BlockSpec((tm, d), row_blk), _resident(w_in_b), _resident(w_qkvt_b),
                  tab_spec, tab_spec, tab_spec, rtab_spec, rtab_spec, _resident(conv_w), st_spec],
        out_specs=[kt_spec, pl.BlockSpec((tm, a), row_blk), kt_spec, kt_spec, vtb_spec,
                   pl.BlockSpec((tm, c), row_blk), st_spec],
        out_shape=out_shape,
        scratch_shapes=[pltpu.VMEM((SUBLANES, c), f32)],
        compiler_params=pltpu.CompilerParams(
            dimension_semantics=("arbitrary", "arbitrary"), vmem_limit_bytes=VMEM_LIMIT),
        name="proj_long",
    )(x2d, w_in_b, w_qkvt_b, *lane_tabs, *row_tabs, conv_w, state8)


def _proj_short(x2d, w_in_b, lane_tabs, conv_w, state8):
    rows, d = x2d.shape
    tm = ROW_TILE_SHORT
    a, c = ATTN_WIDTH, conv_w.shape[1]
    row_blk = lambda t: (t, 0)
    tab_spec = pl.BlockSpec((tm, LANES), lambda t: (0, 0))
    st_spec = pl.BlockSpec((tm // SUBLANES, SUBLANES, c), lambda t: (t, 0, 0))
    f32, bf = jnp.float32, jnp.bfloat16
    out_shape = (
        jax.ShapeDtypeStruct((rows, a), bf), jax.ShapeDtypeStruct((rows, a), f32),
        jax.ShapeDtypeStruct((rows, a), f32), jax.ShapeDtypeStruct((rows, c), bf),
        jax.ShapeDtypeStruct(state8.shape, f32),
    )
    return pl.pallas_call(
        _proj_short_kernel,
        grid=(rows // tm,),
        in_specs=[pl.BlockSpec((tm, d), row_blk), _resident(w_in_b),
                  tab_spec, tab_spec, tab_spec, _resident(conv_w), st_spec],
        out_specs=[pl.BlockSpec((tm, a), row_blk)] * 3 + [pl.BlockSpec((tm, c), row_blk), st_spec],
        out_shape=out_shape,
        compiler_params=pltpu.CompilerParams(
            dimension_semantics=("arbitrary",), vmem_limit_bytes=VMEM_LIMIT),
        name="proj_short",
    )(x2d, w_in_b, *lane_tabs, conv_w, state8)


def _rope_angles(pos):
    half = ROT_DIM // 2
    inv = 1.0 / (ROPE_THETA ** (jnp.arange(half, dtype=jnp.float32) / half))
    ang = pos.astype(jnp.float32)[:, None] * inv[None, :]
    return jnp.cos(ang), jnp.sin(ang)


def _rope_lane_tables(cos, sin):
    t, half = cos.shape
    rest = HEAD_DIM - ROT_DIM
    ones, zeros = jnp.ones((t, rest), jnp.float32), jnp.zeros((t, rest), jnp.float32)
    zh = jnp.zeros((t, half), jnp.float32)
    ta = jnp.concatenate([cos, cos, ones], axis=1)
    tb = jnp.concatenate([-sin, zh, zeros], axis=1)
    tc = jnp.concatenate([zh, sin, zeros], axis=1)
    return tuple(jnp.tile(x, (1, LANES // HEAD_DIM)) for x in (ta, tb, tc))


def _top_mask(gate, n_valid, axis):
    idx = lax.broadcasted_iota(jnp.int32, gate.shape, axis)
    n = gate.shape[axis]
    g = jnp.where(idx < n_valid, gate, -jnp.inf)
    sel = jnp.zeros(gate.shape, jnp.bool_)
    for _ in range(MOBA_TOPK):
        mx = jnp.max(g, axis=axis, keepdims=True)
        first = jnp.min(jnp.where(g == mx, idx, n), axis=axis, keepdims=True)
        pick = (idx == first) & (mx > -jnp.inf)
        sel = sel | pick
        g = jnp.where(pick, -jnp.inf, g)
    return sel


def _moba_prompt_kernel(qt_ref, k_ref, vt_ref, o_ref, km_ref, m_ref, acc_ref):
    i = pl.program_id(2)
    blk = MOBA_BLOCK
    nb = vt_ref.shape[1]
    nbp = -(-nb // BF16_ROWS) * BF16_ROWS
    f32, bf = jnp.float32, jnp.bfloat16

    @pl.when(i == 0)
    def _():
        kf = k_ref[0].astype(f32).reshape(nb, blk, LANES)
        km_ref[...] = jnp.zeros(km_ref.shape, f32)
        km_ref[0:nb, :] = jnp.sum(kf, axis=1) * (1.0 / blk)

    row_q = lax.broadcasted_iota(jnp.int32, (LANES, blk), 0)
    head0 = _bf16(jnp.where(row_q < HEAD_DIM, 1.0, 0.0))
    head1 = _bf16(jnp.where(row_q < HEAD_DIM, 0.0, 1.0))
    qt = qt_ref[0]
    q2 = jnp.concatenate([qt * head0, qt * head1], axis=1)
    gate = _dot(km_ref[...], q2.astype(f32))[0:nbp]
    row_g = lax.broadcasted_iota(jnp.int32, gate.shape, 0)
    visible = _top_mask(gate, i, 0) | (row_g == i)
    bias = _bf16(jnp.where(visible, 0.0, MASK_BIAS))
    q_aug = jnp.concatenate([q2, bias, jnp.zeros((LANES - nbp, 2 * blk), bf)], axis=0)

    lane_k = lax.broadcasted_iota(jnp.int32, (blk, LANES), 1)
    ones_rows = jnp.ones((BF16_ROWS, blk), bf)

    def block(j):
        start = pl.multiple_of(j * blk, blk)
        k_aug = jnp.concatenate([k_ref[0, pl.ds(start, blk), :],
                                 _bf16(jnp.where(lane_k == j, 1.0, 0.0))], axis=1)
        st = _dot(k_aug, q_aug)
        vv = jnp.concatenate([vt_ref[0, j], ones_rows], axis=0)
        return st, vv

    def fold(state, sts, vvs):
        m_new = functools.reduce(jnp.maximum, [jnp.max(st, axis=0, keepdims=True) for st in sts])
        if state is not None:
            m_new = jnp.maximum(state[0], m_new)
        pv = functools.reduce(jnp.add, [_dot(vv, _bf16(jnp.exp(st - m_new))) for st, vv in zip(sts, vvs)])
        if state is not None:
            pv = state[1] * jnp.exp(state[0] - m_new) + pv
        return m_new, pv

    per_trip = PAST_BLOCKS_PER_TRIP
    n_tail = i % per_trip
    for k in range(per_trip):
        @pl.when(n_tail == k)
        def _(k=k):
            st, vv = block(i)
            key = lax.broadcasted_iota(jnp.int32, st.shape, 0)
            col = lax.broadcasted_iota(jnp.int32, st.shape, 1)
            qpos = jnp.where(col >= blk, col - blk, col)
            sts, vvs = [jnp.where(key <= qpos, st, -jnp.inf)], [vv]
            for u in range(k):
                st, vv = block(i - k + u)
                sts.append(st)
                vvs.append(vv)
            m_ref[...], acc_ref[...] = fold(None, sts, vvs)

    def body(t, carry):
        sts, vvs = zip(*[block(per_trip * t + u) for u in range(per_trip)])
        m_ref[...], acc_ref[...] = fold((m_ref[...], acc_ref[...]), sts, vvs)
        return carry

    lax.fori_loop(0, i // per_trip, body, 0)

    acc = acc_ref[...]
    ot = acc[0:LANES] / acc[LANES:LANES + 1]
    x = jnp.concatenate([ot[0:HEAD_DIM, 0:blk], ot[HEAD_DIM:LANES, blk:2 * blk]], axis=0)
    o_ref[0] = _bf16(x.T)


def _moba_prompt(qt, k_rm, vtb, bsz, seq):
    blk = MOBA_BLOCK
    nb = seq // blk
    assert nb <= LANES
    k3 = k_rm.reshape(bsz, seq, ATTN_WIDTH)
    qt_spec = pl.BlockSpec((1, LANES, blk), lambda b, h, i: (b, h, i))
    k_spec = pl.BlockSpec((1, seq, LANES), lambda b, h, i: (b, 0, h))
    vt_spec = pl.BlockSpec((1, nb, LANES, blk), lambda b, h, i: (b, 0, h, 0))
    out = pl.pallas_call(
        _moba_prompt_kernel,
        grid=(bsz, ATTN_WIDTH // LANES, nb),
        in_specs=[qt_spec, k_spec, vt_spec],
        out_specs=pl.BlockSpec((1, blk, LANES), lambda b, h, i: (b, i, h)),
        out_shape=jax.ShapeDtypeStruct((bsz, seq, ATTN_WIDTH), jnp.bfloat16),
        scratch_shapes=[pltpu.VMEM((LANES, LANES), jnp.float32),
                        pltpu.VMEM((1, 2 * blk), jnp.float32),
                        pltpu.VMEM((LANES + BF16_ROWS, 2 * blk), jnp.float32)],
        compiler_params=pltpu.CompilerParams(
            dimension_semantics=("arbitrary", "arbitrary", "arbitrary"), vmem_limit_bytes=VMEM_LIMIT),
        name="moba_prompt",
    )(qt, k3, vtb)
    return out.reshape(bsz * seq, ATTN_WIDTH)


def _moba_sample_kernel(pt_ref, q_ref, k1_ref, v1_ref, ck_ref, cv_ref, o_ref,
                        buf_ref, sem_ref, wq_ref, s_ref, bias_ref, gate_ref, bmax_ref, m_ref, l_ref, acc_ref):
    seq, g = pl.program_id(0), pl.program_id(1)
    steps = pl.num_programs(1)
    ks = s_ref.shape[0]
    n_slot, n_pg = buf_ref.shape[0], buf_ref.shape[1]
    step = seq * steps + g
    total_steps = pl.num_programs(0) * steps

    def page_copy(src_ref, page_id, slot, p):
        return pltpu.make_async_copy(src_ref.at[page_id], buf_ref.at[slot, p], sem_ref.at[slot])

    def fetch(ahead):
        g_raw = g + ahead
        wrap = (g_raw >= steps).astype(jnp.int32)
        seq_f, g_f = seq + wrap, g_raw - wrap * steps
        slot = (step + ahead) % n_slot

        @pl.when(g_f < ks)
        def _():
            for p in range(n_pg):
                page_copy(ck_ref, pt_ref[seq_f, g_f * n_pg + p], slot, p).start()

        @pl.when(g_f >= ks)
        def _():
            for p in range(n_pg):
                page_copy(cv_ref, pt_ref[seq_f, (g_f - ks) * n_pg + p], slot, p).start()

    @pl.when(step == 0)
    def _():
        for ahead in range(PAGE_LOOKAHEAD):
            fetch(ahead)

    @pl.when(step + PAGE_LOOKAHEAD < total_steps)
    def _():
        fetch(PAGE_LOOKAHEAD)

    slot = step % n_slot
    for p in range(n_pg):
        page_copy(ck_ref, 0, slot, p).wait()
    k_pages = v_pages = [buf_ref.at[slot, p] for p in range(n_pg)]

    f32 = jnp.float32
    t_new = q_ref.shape[1]
    rows = N_HEADS * t_new
    page = buf_ref.shape[3]
    blk = MOBA_BLOCK
    pages_per_blk = blk // page
    blks_per_step = n_pg // pages_per_blk
    nblk = ks * blks_per_step
    row = lax.broadcasted_iota(jnp.int32, (rows, ATTN_WIDTH), 0)
    lane = lax.broadcasted_iota(jnp.int32, (rows, ATTN_WIDTH), 1)
    own_head = (row // t_new) == (lane // HEAD_DIM)

    @pl.when(g == 0)
    def _():
        q8 = q_ref[0].astype(f32)
        wq_ref[...] = _bf16(jnp.where(own_head, jnp.concatenate([q8] * N_HEADS, axis=0), 0.0))
        gate_ref[...] = jnp.zeros(gate_ref.shape, f32)
        bmax_ref[...] = jnp.zeros(bmax_ref.shape, f32)

    @pl.when(g < ks)
    def _():
        wq = wq_ref[...]
        lane_g = lax.broadcasted_iota(jnp.int32, (rows, LANES), 1)
        gate, bmax = gate_ref[...], bmax_ref[...]
        for b in range(blks_per_step):
            pg = range(b * pages_per_blk, (b + 1) * pages_per_blk)
            s_blk = jnp.concatenate([_dot(wq, _bf16(k_pages[p][...])) for p in pg], axis=1)
            s_ref[g, :, b * blk:(b + 1) * blk] = s_blk
            hit = lane_g == g * blks_per_step + b
            gate = jnp.where(hit, jnp.sum(s_blk, axis=-1, keepdims=True), gate)
            bmax = jnp.where(hit, jnp.max(s_blk, axis=-1, keepdims=True), bmax)
        gate_ref[...] = gate
        bmax_ref[...] = bmax

    @pl.when(g == ks - 1)
    def _():
        wq = wq_ref[...]
        sel = _top_mask(gate_ref[...], nblk, 1)
        bias = jnp.where(sel, 0.0, -jnp.inf)
        for gi in range(ks):
            shift = (LANES - gi * blks_per_step) % LANES
            bias_ref[gi] = pltpu.roll(bias, shift, 1) if shift else bias
        zpad = jnp.zeros((LANES - t_new, ATTN_WIDTH), jnp.bfloat16)
        k_own = jnp.concatenate([_bf16(k1_ref[0]), zpad], axis=0)
        v_own = jnp.concatenate([_bf16(v1_ref[0]), zpad], axis=0)
        s_own = _dot_nt(wq, k_own)
        r_o = lax.broadcasted_iota(jnp.int32, s_own.shape, 0)
        c_o = lax.broadcasted_iota(jnp.int32, s_own.shape, 1)
        s_own = jnp.where(c_o <= r_o % t_new, s_own, -jnp.inf)
        m = jnp.maximum(jnp.max(jnp.where(sel, bmax_ref[...], -jnp.inf), axis=-1, keepdims=True),
                        jnp.max(s_own, axis=-1, keepdims=True))
        m_ref[...] = m
        p_own = jnp.exp(s_own - m)
        l_ref[...] = jnp.sum(p_own, axis=-1, keepdims=True)
        acc_ref[...] = _dot(_bf16(p_own), v_own)

    @pl.when(g >= ks)
    def _():
        gv = g - ks
        m, bias = m_ref[...], bias_ref[gv]
        den = jnp.zeros((rows, blk), f32)
        o = acc_ref[...]
        for b in range(blks_per_step):
            p_blk = jnp.exp(s_ref[gv, :, b * blk:(b + 1) * blk] + bias[:, b:b + 1] - m)
            den = den + p_blk
            pb = _bf16(p_blk)
            for x in range(pages_per_blk):
                o = o + _dot_nt(pb[:, x * page:(x + 1) * page], _bf16(v_pages[b * pages_per_blk + x][...]))
        acc_ref[...] = o
        l_ref[...] = l_ref[...] + jnp.sum(den, axis=-1, keepdims=True)

    @pl.when(g == 2 * ks - 1)
    def _():
        o64 = jnp.where(own_head, acc_ref[...] / l_ref[...], 0.0)
        o8 = o64[0:t_new]
        for h in range(1, N_HEADS):
            o8 = o8 + o64[h * t_new:(h + 1) * t_new]
        o_ref[0] = _bf16(o8)


def _moba_sample(qb, k_new, v_new, cache_k, cache_v, page_table, n_seq, t_new):
    n_phys, page = cache_k.shape[0], cache_k.shape[1]
    ck = cache_k.transpose(0, 2, 3, 1).reshape(n_phys, ATTN_WIDTH, page)
    cv = cache_v.transpose(0, 2, 3, 1).reshape(n_phys, ATTN_WIDTH, page)
    n_pages = page_table.shape[1]
    pps = PAGES_PER_STEP
    ks = n_pages // pps
    rows = N_HEADS * t_new
    assert n_pages * page // MOBA_BLOCK <= LANES
    assert PAGE_LOOKAHEAD < PAGE_SLOTS and PAGE_LOOKAHEAD <= 2 * ks
    seq_spec = pl.BlockSpec((1, t_new, ATTN_WIDTH), lambda s, g, pt: (s, 0, 0))
    hbm_spec = pl.BlockSpec(memory_space=pl.ANY)

    f32, bf = jnp.float32, jnp.bfloat16
    out = pl.pallas_call(
        _moba_sample_kernel,
        grid_spec=pltpu.PrefetchScalarGridSpec(
            num_scalar_prefetch=1,
            grid=(n_seq, 2 * ks),
            in_specs=[seq_spec, seq_spec, seq_spec, hbm_spec, hbm_spec],
            out_specs=seq_spec,
            scratch_shapes=[pltpu.VMEM((PAGE_SLOTS, pps, ATTN_WIDTH, page), f32),
                            pltpu.SemaphoreType.DMA((PAGE_SLOTS,)),
                            pltpu.VMEM((rows, ATTN_WIDTH), bf),
                            pltpu.VMEM((ks, rows, pps * page), f32),
                            pltpu.VMEM((ks, rows, LANES), f32),
                            pltpu.VMEM((rows, LANES), f32),
                            pltpu.VMEM((rows, LANES), f32),
                            pltpu.VMEM((rows, 1), f32),
                            pltpu.VMEM((rows, 1), f32),
                            pltpu.VMEM((rows, ATTN_WIDTH), f32)]),
        out_shape=jax.ShapeDtypeStruct((n_seq, t_new, ATTN_WIDTH), bf),
        compiler_params=pltpu.CompilerParams(
            dimension_semantics=("arbitrary", "arbitrary"), vmem_limit_bytes=VMEM_LIMIT),
        name="moba_sample",
    )(page_table, qb.reshape(n_seq, t_new, ATTN_WIDTH), k_new.reshape(n_seq, t_new, ATTN_WIDTH),
      v_new.reshape(n_seq, t_new, ATTN_WIDTH), ck, cv)
    return out.reshape(n_seq * t_new, ATTN_WIDTH)


def _post_kernel(x_ref, at_ref, cg_ref, st_ref, wo_ref, g1_ref, b1_ref, wup_ref, fw_ref, fb_ref, wdn_ref,
                 g2_ref, b2_ref, y_ref, tail_ref, carry_ref, hid_ref, *, long_seq, alpha):
    rows = x_ref.shape[0]
    d_ff = wdn_ref.shape[0]
    mix = _dot(jnp.concatenate([at_ref[...], cg_ref[...]], axis=1), wo_ref[...])
    x1 = _layer_norm(alpha * x_ref[...] + mix, g1_ref[...], b1_ref[...])
    x1b = _bf16(x1)

    if long_seq:
        @pl.when(pl.program_id(1) == 0)
        def _():
            carry_ref[...] = st_ref[0]

    for c0 in range(0, d_ff, FF_CHUNK):
        cols = slice(c0, c0 + FF_CHUNK)
        up_a = _dot(x1b, wup_ref[:, cols])
        up_g = _dot(x1b, wup_ref[:, d_ff + c0:d_ff + c0 + FF_CHUNK])
        a3 = up_a.reshape(rows // SUBLANES, SUBLANES, FF_CHUNK)
        if long_seq:
            prev3 = _prev_groups(a3, carry_ref[:, cols])
            carry_ref[:, cols] = a3[-1]
            tail_ref[0, :, cols] = a3[-1]
        else:
            prev3 = st_ref[:, :, cols]
            tail_ref[:, :, cols] = a3
        a_c = (_causal_conv3(a3, prev3, fw_ref, cols) + fb_ref[0:1, cols]).reshape(rows, FF_CHUNK)
        act = 0.5 * a_c * (1.0 + lax.erf(a_c * np.float32(np.sqrt(0.5))))
        hid_ref[:, cols] = _bf16(act * up_g)
    f = _dot(hid_ref[...], wdn_ref[...])
    y_ref[...] = _layer_norm(alpha * x1 + f, g2_ref[...], b2_ref[...])


def _post(x2d, attn, cg, state8, w_o_b, ln1_g, ln1_b, w_up_b, fcw, fcb, w_dn_b, ln2_g, ln2_b,
          *, nb, nt, long_seq, alpha):
    rows, d = x2d.shape
    tm = rows // (nb * nt)
    d_ff = w_dn_b.shape[0]
    row_blk = lambda b, t: (b * nt + t, 0)
    if long_seq:
        st_spec = pl.BlockSpec((1, SUBLANES, d_ff), lambda b, t: (b, 0, 0))
    else:
        st_spec = pl.BlockSpec((tm // SUBLANES, SUBLANES, d_ff), lambda b, t: (t, 0, 0))
    vec = lambda v: v.reshape(1, -1)
    ln1_g, ln1_b, fcb, ln2_g, ln2_b = (vec(v) for v in (ln1_g, ln1_b, fcb, ln2_g, ln2_b))
    consts = (w_o_b, ln1_g, ln1_b, w_up_b, fcw, fcb, w_dn_b, ln2_g, ln2_b)
    return pl.pallas_call(
        functools.partial(_post_kernel, long_seq=long_seq, alpha=alpha),
        grid=(nb, nt),
        in_specs=[pl.BlockSpec((tm, d), row_blk),
                  pl.BlockSpec((tm, attn.shape[1]), row_blk),
                  pl.BlockSpec((tm, cg.shape[1]), row_blk),
                  st_spec] + [_resident(w) for w in consts],
        out_specs=[pl.BlockSpec((tm, d), row_blk), st_spec],
        out_shape=(jax.ShapeDtypeStruct((rows, d), jnp.float32),
                   jax.ShapeDtypeStruct(state8.shape, jnp.float32)),
        scratch_shapes=[pltpu.VMEM((SUBLANES, d_ff), jnp.float32),
                        pltpu.VMEM((tm, d_ff), jnp.bfloat16)],
        compiler_params=pltpu.CompilerParams(
            dimension_semantics=("arbitrary", "arbitrary"), vmem_limit_bytes=VMEM_LIMIT),
        name="post_long" if long_seq else "post_short",
    )(x2d, attn, cg, state8, *consts)


def _pad_state(state):
    return jnp.pad(state, ((0, 0), (SUBLANES - (CONV_K - 1), 0), (0, 0)))


def kernel(x_prompt, x_sample, cache_k, cache_v, page_table, state_conv, state_ffn_conv, w_in, conv_w, w_o,
           ln1_g, ln1_b, w_up, ffn_conv_w, ffn_conv_b, w_down, ln2_g, ln2_b):
    depth = w_in.shape[0]
    bsz, seq, d_model = x_prompt.shape
    n_seq, t_new, _ = x_sample.shape
    conv_width = conv_w.shape[2]
    d_ff = w_down.shape[1]
    past_len = page_table.shape[1] * cache_k.shape[2]
    alpha = (2.0 * depth) ** 0.25
    a = ATTN_WIDTH
    assert t_new == SUBLANES and seq % ROW_TILE == 0 and (n_seq * t_new) % ROW_TILE_SHORT == 0
    assert past_len % MOBA_BLOCK == 0 and past_len // MOBA_BLOCK >= MOBA_TOPK and d_ff % FF_CHUNK == 0
    assert page_table.shape[1] % PAGES_PER_STEP == 0 and (PAGES_PER_STEP * cache_k.shape[2]) % MOBA_BLOCK == 0

    cos_p, sin_p = _rope_angles(jnp.arange(seq, dtype=jnp.int32))
    cos_s, sin_s = _rope_angles(past_len + jnp.arange(t_new, dtype=jnp.int32))
    lane_tabs_p = _rope_lane_tables(cos_p, sin_p)
    row_tabs_p = (cos_p.T, sin_p.T)
    lane_tabs_s = tuple(jnp.tile(t, (ROW_TILE_SHORT // t_new, 1)) for t in _rope_lane_tables(cos_s, sin_s))
    grid_p = dict(nb=bsz, nt=seq // ROW_TILE, long_seq=True)
    grid_s = dict(nb=1, nt=n_seq * t_new // ROW_TILE_SHORT, long_seq=False)
    zeros_conv = jnp.zeros((bsz, SUBLANES, conv_width), jnp.float32)
    zeros_ffn = jnp.zeros((bsz, SUBLANES, d_ff), jnp.float32)

    xp = x_prompt.reshape(bsz * seq, d_model)
    xs = x_sample.reshape(n_seq * t_new, d_model)
    outs = [[] for _ in range(8)]
    for l in range(depth):
        w_in_b, w_o_b, w_up_b, w_dn_b = _bf16(w_in[l]), _bf16(w_o[l]), _bf16(w_up[l]), _bf16(w_down[l])
        w_qkvt_b = w_in_b[:, 0:3 * a].T
        post_w = (w_o_b, ln1_g[l], ln1_b[l], w_up_b, ffn_conv_w[l], ffn_conv_b[l], w_dn_b, ln2_g[l], ln2_b[l])

        qt, k_rm, kt, vt, vtb, cg, tail = _proj_long(
            xp, w_in_b, w_qkvt_b, lane_tabs_p, row_tabs_p, conv_w[l], zeros_conv, bsz, seq)
        attn = _moba_prompt(qt, k_rm, vtb, bsz, seq)
        xp, ffn_tail = _post(xp, attn, cg, zeros_ffn, *post_w, alpha=alpha, **grid_p)
        new = (kt.reshape(bsz, N_HEADS, HEAD_DIM, seq).transpose(0, 3, 1, 2),
               vt.reshape(bsz, N_HEADS, HEAD_DIM, seq).transpose(0, 3, 1, 2),
               tail[:, -(CONV_K - 1):], ffn_tail[:, -(CONV_K - 1):])
        for dst, val in zip(outs[0::2], new):
            dst.append(val)

        qb, k, v, cg, tail = _proj_short(xs, w_in_b, lane_tabs_s, conv_w[l], _pad_state(state_conv[l]))
        attn = _moba_sample(qb, k, v, cache_k[l], cache_v[l], page_table, n_seq, t_new)
        xs, ffn_tail = _post(xs, attn, cg, _pad_state(state_ffn_conv[l]), *post_w, alpha=alpha, **grid_s)
        new = (k.reshape(n_seq, t_new, N_HEADS, HEAD_DIM), v.reshape(n_seq, t_new, N_HEADS, HEAD_DIM),
               tail[:, -(CONV_K - 1):], ffn_tail[:, -(CONV_K - 1):])
        for dst, val in zip(outs[1::2], new):
            dst.append(val)

    kp, ks, vp, vs, cp, cs, fp, fs = (jnp.stack(o) for o in outs)
    return (xp.reshape(bsz, seq, d_model), xs.reshape(n_seq, t_new, d_model), kp, vp, ks, vs, cp, cs, fp, fs)
```

```python
import functools

import jax
import jax.numpy as jnp
import numpy as np
from jax import lax
from jax.experimental import pallas as pl
from jax.experimental.pallas import tpu as pltpu

N_HEADS = 8
HEAD_DIM = 64
ATTN_WIDTH = N_HEADS * HEAD_DIM
ROT_DIM = HEAD_DIM // 4
ROPE_THETA = 500000.0
MOBA_BLOCK = 256
MOBA_TOPK = 3
CONV_K = 3
LN_EPS = 1e-5

LANES = 128
SUBLANES = 8
BF16_ROWS = 16
ROW_TILE = 512
ROW_TILE_SHORT = 256
FF_CHUNK = 256
PAGES_PER_STEP = 32
PAGE_SLOTS = 3
PAGE_LOOKAHEAD = 2
PAST_BLOCKS_PER_TRIP = 4
HEAD_TILES_PER_STEP = 4
VMEM_LIMIT = 56 * 1024 * 1024
MASK_BIAS = -1e30

_NT = (((1,), (1,)), ((), ()))


def _bf16(x):
    return x.astype(jnp.bfloat16)


def _dot(a, b):
    return jnp.dot(a, b, preferred_element_type=jnp.float32)


def _dot_nt(a, b):
    return lax.dot_general(a, b, _NT, preferred_element_type=jnp.float32)


def _layer_norm(x, g, b):
    mu = jnp.mean(x, axis=-1, keepdims=True)
    xc = x - mu
    var = jnp.mean(xc * xc, axis=-1, keepdims=True)
    return xc * lax.rsqrt(var + LN_EPS) * g + b


def _resident(arr):
    return pl.BlockSpec(arr.shape, lambda *_: (0,) * arr.ndim, pipeline_mode=pl.Buffered(1))


def _causal_conv3(u3, prev3, w_ref, cols):
    sub = lax.broadcasted_iota(jnp.int32, u3.shape, 1)
    s1 = jnp.where(sub < 1, pltpu.roll(prev3, 1, 1), pltpu.roll(u3, 1, 1))
    s2 = jnp.where(sub < 2, pltpu.roll(prev3, 2, 1), pltpu.roll(u3, 2, 1))
    return s2 * w_ref[0:1, cols] + s1 * w_ref[1:2, cols] + u3 * w_ref[2:3, cols]


def _prev_groups(u3, carry):
    return jnp.concatenate([carry[None], u3[:-1]], axis=0)


def _rope_lanes(z, ra_ref, rb_ref, rc_ref):
    outs = []
    for c in range(z.shape[1] // LANES):
        zc = z[:, c * LANES:(c + 1) * LANES]
        outs.append(zc * ra_ref[...]
                    + pltpu.roll(zc, LANES - ROT_DIM // 2, 1) * rb_ref[...]
                    + pltpu.roll(zc, ROT_DIM // 2, 1) * rc_ref[...])
    return jnp.concatenate(outs, axis=1)


def _rope_rows(zt, cos_ref, sin_ref):
    half = ROT_DIM // 2
    cos, sin = cos_ref[...], sin_ref[...]
    parts = []
    for h in range(N_HEADS):
        r0 = h * HEAD_DIM
        x1, x2 = zt[r0:r0 + half], zt[r0 + half:r0 + ROT_DIM]
        parts += [x1 * cos - x2 * sin, x2 * cos + x1 * sin, zt[r0 + ROT_DIM:r0 + HEAD_DIM]]
    return jnp.concatenate(parts, axis=0)


def _gated_conv(xb, w_ref, cw_ref, prev3_fn, cg_ref):
    a = ATTN_WIDTH
    c = cw_ref.shape[1]
    rows = xb.shape[0]
    gb = _dot(xb, w_ref[:, 3 * a:3 * a + c])
    u = _dot(xb, w_ref[:, 3 * a + c:3 * a + 2 * c]) * _dot(xb, w_ref[:, 3 * a + 2 * c:3 * a + 3 * c])
    u3 = u.reshape(rows // SUBLANES, SUBLANES, c)
    y = _causal_conv3(u3, prev3_fn(u3), cw_ref, slice(None)).reshape(rows, c)
    cg_ref[...] = _bf16(gb * y)
    return u3


def _proj_long_kernel(x_ref, w_ref, wt_ref, ra_ref, rb_ref, rc_ref, cos_ref, sin_ref, cw_ref, st_ref,
                      qt_ref, k_ref, kt_ref, vt_ref, vtb_ref, cg_ref, tail_ref, carry_ref):
    a = ATTN_WIDTH
    xb = _bf16(x_ref[...])
    qkvt = _dot_nt(wt_ref[...], xb)
    qt_ref[0] = _bf16(_rope_rows(qkvt[0:a], cos_ref, sin_ref) * (HEAD_DIM ** -0.5))
    kt_ref[0] = _rope_rows(qkvt[a:2 * a], cos_ref, sin_ref)
    vt = qkvt[2 * a:3 * a]
    vt_ref[0] = vt
    for c in range(vtb_ref.shape[1]):
        vtb_ref[0, c] = _bf16(vt[:, c * MOBA_BLOCK:(c + 1) * MOBA_BLOCK])
    k_ref[...] = _bf16(_rope_lanes(_dot(xb, w_ref[:, a:2 * a]), ra_ref, rb_ref, rc_ref))

    @pl.when(pl.program_id(1) == 0)
    def _():
        carry_ref[...] = st_ref[0]

    u3 = _gated_conv(xb, w_ref, cw_ref, lambda u3: _prev_groups(u3, carry_ref[...]), cg_ref)
    carry_ref[...] = u3[-1]
    tail_ref[0] = u3[-1]


def _proj_short_kernel(x_ref, w_ref, ra_ref, rb_ref, rc_ref, cw_ref, st_ref,
                       q_ref, k_ref, v_ref, cg_ref, tail_ref):
    a = ATTN_WIDTH
    xb = _bf16(x_ref[...])
    q = _rope_lanes(_dot(xb, w_ref[:, 0:a]), ra_ref, rb_ref, rc_ref)
    q_ref[...] = _bf16(q * (HEAD_DIM ** -0.5))
    k_ref[...] = _rope_lanes(_dot(xb, w_ref[:, a:2 * a]), ra_ref, rb_ref, rc_ref)
    v_ref[...] = _dot(xb, w_ref[:, 2 * a:3 * a])
    tail_ref[...] = _gated_conv(xb, w_ref, cw_ref, lambda u3: st_ref[...], cg_ref)


def _proj_long(x2d, w_in_b, w_qkvt_b, lane_tabs, row_tabs, conv_w, state8, bsz, seq):
    rows, d = x2d.shape
    tm = ROW_TILE
    nt = seq // tm
    a, c = ATTN_WIDTH, conv_w.shape[1]
    nblk = tm // MOBA_BLOCK
    row_blk = lambda b, t: (b * nt + t, 0)
    tab_spec = pl.BlockSpec((tm, LANES), lambda b, t: (t, 0))
    rtab_spec = pl.BlockSpec((ROT_DIM // 2, tm), lambda b, t: (0, t))
    st_spec = pl.BlockSpec((1, SUBLANES, c), lambda b, t: (b, 0, 0))
    kt_spec = pl.BlockSpec((1, a, tm), lambda b, t: (b, 0, t))
    vtb_spec = pl.BlockSpec((1, nblk, a, MOBA_BLOCK), lambda b, t: (b, t, 0, 0))
    f32, bf = jnp.float32, jnp.bfloat16
    out_shape = (
        jax.ShapeDtypeStruct((bsz, a, seq), bf),
        jax.ShapeDtypeStruct((rows, a), bf),
        jax.ShapeDtypeStruct((bsz, a, seq), f32), jax.ShapeDtypeStruct((bsz, a, seq), f32),
        jax.ShapeDtypeStruct((bsz, seq // MOBA_BLOCK, a, MOBA_BLOCK), bf),
        jax.ShapeDtypeStruct((rows, c), bf),
        jax.ShapeDtypeStruct(state8.shape, f32),
    )
    return pl.pallas_call(
        _proj_long_kernel,
        grid=(bsz, nt),
        in_specs=[pl.BlockSpec((tm, d), row_blk), _resident(w_in_b), _resident(w_qkvt_b),
                  tab_spec, tab_spec, tab_spec, rtab_spec, rtab_spec, _resident(conv_w), st_spec],
        out_specs=[kt_spec, pl.BlockSpec((tm, a), row_blk), kt_spec, kt_spec, vtb_spec,
                   pl.BlockSpec((tm, c), row_blk), st_spec],
        out_shape=out_shape,
        scratch_shapes=[pltpu.VMEM((SUBLANES, c), f32)],
        compiler_params=pltpu.CompilerParams(
            dimension_semantics=("arbitrary", "arbitrary"), vmem_limit_bytes=VMEM_LIMIT),
        name="proj_long",
    )(x2d, w_in_b, w_qkvt_b, *lane_tabs, *row_tabs, conv_w, state8)


def _proj_short(x2d, w_in_b, lane_tabs, conv_w, state8):
    rows, d = x2d.shape
    tm = ROW_TILE_SHORT
    a, c = ATTN_WIDTH, conv_w.shape[1]
    row_blk = lambda t: (t, 0)
    tab_spec = pl.BlockSpec((tm, LANES), lambda t: (0, 0))
    st_spec = pl.BlockSpec((tm // SUBLANES, SUBLANES, c), lambda t: (t, 0, 0))
    f32, bf = jnp.float32, jnp.bfloat16
    out_shape = (
        jax.ShapeDtypeStruct((rows, a), bf), jax.ShapeDtypeStruct((rows, a), f32),
        jax.ShapeDtypeStruct((rows, a), f32), jax.ShapeDtypeStruct((rows, c), bf),
        jax.ShapeDtypeStruct(state8.shape, f32),
    )
    return pl.pallas_call(
        _proj_short_kernel,
        grid=(rows // tm,),
        in_specs=[pl.BlockSpec((tm, d), row_blk), _resident(w_in_b),
                  tab_spec, tab_spec, tab_spec, _resident(conv_w), st_spec],
        out_specs=[pl.BlockSpec((tm, a), row_blk)] * 3 + [pl.BlockSpec((tm, c), row_blk), st_spec],
        out_shape=out_shape,
        compiler_params=pltpu.CompilerParams(
            dimension_semantics=("arbitrary",), vmem_limit_bytes=VMEM_LIMIT),
        name="proj_short",
    )(x2d, w_in_b, *lane_tabs, conv_w, state8)


def _rope_angles(pos):
    half = ROT_DIM // 2
    inv = 1.0 / (ROPE_THETA ** (jnp.arange(half, dtype=jnp.float32) / half))
    ang = pos.astype(jnp.float32)[:, None] * inv[None, :]
    return jnp.cos(ang), jnp.sin(ang)


def _rope_lane_tables(cos, sin):
    t, half = cos.shape
    rest = HEAD_DIM - ROT_DIM
    ones, zeros = jnp.ones((t, rest), jnp.float32), jnp.zeros((t, rest), jnp.float32)
    zh = jnp.zeros((t, half), jnp.float32)
    ta = jnp.concatenate([cos, cos, ones], axis=1)
    tb = jnp.concatenate([-sin, zh, zeros], axis=1)
    tc = jnp.concatenate([zh, sin, zeros], axis=1)
    return tuple(jnp.tile(x, (1, LANES // HEAD_DIM)) for x in (ta, tb, tc))


def _top_mask(gate, n_valid, axis):
    idx = lax.broadcasted_iota(jnp.int32, gate.shape, axis)
    n = gate.shape[axis]
    g = jnp.where(idx < n_valid, gate, -jnp.inf)
    sel = jnp.zeros(gate.shape, jnp.bool_)
    for _ in range(MOBA_TOPK):
        mx = jnp.max(g, axis=axis, keepdims=True)
        first = jnp.min(jnp.where(g == mx, idx, n), axis=axis, keepdims=True)
        pick = (idx == first) & (mx > -jnp.inf)
        sel = sel | pick
        g = jnp.where(pick, -jnp.inf, g)
    return sel


def _moba_prompt_kernel(qt_ref, k_ref, vt_ref, o_ref, km_ref, m_ref, acc_ref):
    i = pl.program_id(2)
    blk = MOBA_BLOCK
    nb = vt_ref.shape[1]
    tiles = km_ref.shape[0]
    nbp = -(-nb // BF16_ROWS) * BF16_ROWS
    f32, bf = jnp.float32, jnp.bfloat16

    @pl.when(i == 0)
    def _():
        for ti in range(tiles):
            kf = k_ref[0, :, ti * LANES:(ti + 1) * LANES].astype(f32).reshape(nb, blk, LANES)
            km_ref[ti] = jnp.zeros((LANES, LANES), f32)
            km_ref[ti, 0:nb, :] = jnp.sum(kf, axis=1) * (1.0 / blk)

    row_q = lax.broadcasted_iota(jnp.int32, (LANES, blk), 0)
    head0 = _bf16(jnp.where(row_q < HEAD_DIM, 1.0, 0.0))
    head1 = _bf16(jnp.where(row_q < HEAD_DIM, 0.0, 1.0))
    lane_k = lax.broadcasted_iota(jnp.int32, (blk, LANES), 1)
    ones_rows = jnp.ones((BF16_ROWS, blk), bf)

    def make_block_fn(ti):
        lanes = slice(ti * LANES, (ti + 1) * LANES)
        qt = qt_ref[0, lanes, :]
        q2 = jnp.concatenate([qt * head0, qt * head1], axis=1)
        gate = _dot(km_ref[ti], q2.astype(f32))[0:nbp]
        row_g = lax.broadcasted_iota(jnp.int32, gate.shape, 0)
        visible = _top_mask(gate, i, 0) | (row_g == i)
        bias = _bf16(jnp.where(visible, 0.0, MASK_BIAS))
        q_aug = jnp.concatenate([q2, bias, jnp.zeros((LANES - nbp, 2 * blk), bf)], axis=0)

        def block(j):
            start = pl.multiple_of(j * blk, blk)
            k_aug = jnp.concatenate([k_ref[0, pl.ds(start, blk), lanes],
                                     _bf16(jnp.where(lane_k == j, 1.0, 0.0))], axis=1)
            st = _dot(k_aug, q_aug)
            vv = jnp.concatenate([vt_ref[0, j, lanes, :], ones_rows], axis=0)
            return st, vv

        return block

    block_fns = [make_block_fn(ti) for ti in range(tiles)]

    def fold(state, sts, vvs):
        m_new = functools.reduce(jnp.maximum, [jnp.max(st, axis=0, keepdims=True) for st in sts])
        if state is not None:
            m_new = jnp.maximum(state[0], m_new)
        pv = functools.reduce(jnp.add, [_dot(vv, _bf16(jnp.exp(st - m_new))) for st, vv in zip(sts, vvs)])
        if state is not None:
            pv = state[1] * jnp.exp(state[0] - m_new) + pv
        return m_new, pv

    per_trip = PAST_BLOCKS_PER_TRIP
    n_tail = i % per_trip
    for k in range(per_trip):
        @pl.when(n_tail == k)
        def _(k=k):
            for ti, block in enumerate(block_fns):
                st, vv = block(i)
                key = lax.broadcasted_iota(jnp.int32, st.shape, 0)
                col = lax.broadcasted_iota(jnp.int32, st.shape, 1)
                qpos = jnp.where(col >= blk, col - blk, col)
                sts, vvs = [jnp.where(key <= qpos, st, -jnp.inf)], [vv]
                for u in range(k):
                    st, vv = block(i - k + u)
                    sts.append(st)
                    vvs.append(vv)
                m_ref[ti], acc_ref[ti] = fold(None, sts, vvs)

    def body(t, carry):
        for ti, block in enumerate(block_fns):
            sts, vvs = zip(*[block(per_trip * t + u) for u in range(per_trip)])
            m_ref[ti], acc_ref[ti] = fold((m_ref[ti], acc_ref[ti]), sts, vvs)
        return carry

    lax.fori_loop(0, i // per_trip, body, 0)

    for ti in range(tiles):
        acc = acc_ref[ti]
        ot = acc[0:LANES] / acc[LANES:LANES + 1]
        x = jnp.concatenate([ot[0:HEAD_DIM, 0:blk], ot[HEAD_DIM:LANES, blk:2 * blk]], axis=0)
        o_ref[0, :, ti * LANES:(ti + 1) * LANES] = _bf16(x.T)


def _moba_prompt(qt, k_rm, vtb, bsz, seq):
    blk = MOBA_BLOCK
    nb = seq // blk
    tiles = HEAD_TILES_PER_STEP
    width = tiles * LANES
    assert nb <= LANES and ATTN_WIDTH % width == 0
    k3 = k_rm.reshape(bsz, seq, ATTN_WIDTH)
    qt_spec = pl.BlockSpec((1, width, blk), lambda b, h, i: (b, h, i))
    k_spec = pl.BlockSpec((1, seq, width), lambda b, h, i: (b, 0, h))
    vt_spec = pl.BlockSpec((1, nb, width, blk), lambda b, h, i: (b, 0, h, 0))
    out = pl.pallas_call(
        _moba_prompt_kernel,
        grid=(bsz, ATTN_WIDTH // width, nb),
        in_specs=[qt_spec, k_spec, vt_spec],
        out_specs=pl.BlockSpec((1, blk, width), lambda b, h, i: (b, i, h)),
        out_shape=jax.ShapeDtypeStruct((bsz, seq, ATTN_WIDTH), jnp.bfloat16),
        scratch_shapes=[pltpu.VMEM((tiles, LANES, LANES), jnp.float32),
                        pltpu.VMEM((tiles, 1, 2 * blk), jnp.float32),
                        pltpu.VMEM((tiles, LANES + BF16_ROWS, 2 * blk), jnp.float32)],
        compiler_params=pltpu.CompilerParams(
            dimension_semantics=("arbitrary", "arbitrary", "arbitrary"), vmem_limit_bytes=VMEM_LIMIT),
        name="moba_prompt",
    )(qt, k3, vtb)
    return out.reshape(bsz * seq, ATTN_WIDTH)


def _moba_sample_kernel(pt_ref, q_ref, k1_ref, v1_ref, ck_ref, cv_ref, o_ref,
                        buf_ref, sem_ref, wq_ref, s_ref, bias_ref, gate_ref, bmax_ref, m_ref, l_ref, acc_ref):
    seq, g = pl.program_id(0), pl.program_id(1)
    steps = pl.num_programs(1)
    ks = s_ref.shape[0]
    n_slot, n_pg = buf_ref.shape[0], buf_ref.shape[1]
    step = seq * steps + g
    total_steps = pl.num_programs(0) * steps

    def page_copy(src_ref, page_id, slot, p):
        return pltpu.make_async_copy(src_ref.at[page_id], buf_ref.at[slot, p], sem_ref.at[slot])

    def fetch(ahead):
        g_raw = g + ahead
        wrap = (g_raw >= steps).astype(jnp.int32)
        seq_f, g_f = seq + wrap, g_raw - wrap * steps
        slot = (step + ahead) % n_slot

        @pl.when(g_f < ks)
        def _():
            for p in range(n_pg):
                page_copy(ck_ref, pt_ref[seq_f, g_f * n_pg + p], slot, p).start()

        @pl.when(g_f >= ks)
        def _():
            for p in range(n_pg):
                page_copy(cv_ref, pt_ref[seq_f, (g_f - ks) * n_pg + p], slot, p).start()

    @pl.when(step == 0)
    def _():
        for ahead in range(PAGE_LOOKAHEAD):
            fetch(ahead)

    @pl.when(step + PAGE_LOOKAHEAD < total_steps)
    def _():
        fetch(PAGE_LOOKAHEAD)

    slot = step % n_slot
    for p in range(n_pg):
        page_copy(ck_ref, 0, slot, p).wait()
    k_pages = v_pages = [buf_ref.at[slot, p] for p in range(n_pg)]

    f32 = jnp.float32
    t_new = q_ref.shape[1]
    rows = N_HEADS * t_new
    page = buf_ref.shape[3]
    blk = MOBA_BLOCK
    pages_per_blk = blk // page
    blks_per_step = n_pg // pages_per_blk
    nblk = ks * blks_per_step
    row = lax.broadcasted_iota(jnp.int32, (rows, ATTN_WIDTH), 0)
    lane = lax.broadcasted_iota(jnp.int32, (rows, ATTN_WIDTH), 1)
    own_head = (row // t_new) == (lane // HEAD_DIM)

    @pl.when(g == 0)
    def _():
        q8 = q_ref[0].astype(f32)
        wq_ref[...] = _bf16(jnp.where(own_head, jnp.concatenate([q8] * N_HEADS, axis=0), 0.0))
        gate_ref[...] = jnp.zeros(gate_ref.shape, f32)
        bmax_ref[...] = jnp.zeros(bmax_ref.shape, f32)

    @pl.when(g < ks)
    def _():
        wq = wq_ref[...]
        lane_g = lax.broadcasted_iota(jnp.int32, (rows, LANES), 1)
        gate, bmax = gate_ref[...], bmax_ref[...]
        for b in range(blks_per_step):
            pg = range(b * pages_per_blk, (b + 1) * pages_per_blk)
            s_blk = jnp.concatenate([_dot(wq, _bf16(k_pages[p][...])) for p in pg], axis=1)
            s_ref[g, :, b * blk:(b + 1) * blk] = s_blk
            hit = lane_g == g * blks_per_step + b
            gate = jnp.where(hit, jnp.sum(s_blk, axis=-1, keepdims=True), gate)
            bmax = jnp.where(hit, jnp.max(s_blk, axis=-1, keepdims=True), bmax)
        gate_ref[...] = gate
        bmax_ref[...] = bmax

    @pl.when(g == ks - 1)
    def _():
        wq = wq_ref[...]
        sel = _top_mask(gate_ref[...], nblk, 1)
        bias = jnp.where(sel, 0.0, -jnp.inf)
        for gi in range(ks):
            shift = (LANES - gi * blks_per_step) % LANES
            bias_ref[gi] = pltpu.roll(bias, shift, 1) if shift else bias
        zpad = jnp.zeros((LANES - t_new, ATTN_WIDTH), jnp.bfloat16)
        k_own = jnp.concatenate([_bf16(k1_ref[0]), zpad], axis=0)
        v_own = jnp.concatenate([_bf16(v1_ref[0]), zpad], axis=0)
        s_own = _dot_nt(wq, k_own)
        r_o = lax.broadcasted_iota(jnp.int32, s_own.shape, 0)
        c_o = lax.broadcasted_iota(jnp.int32, s_own.shape, 1)
        s_own = jnp.where(c_o <= r_o % t_new, s_own, -jnp.inf)
        m = jnp.maximum(jnp.max(jnp.where(sel, bmax_ref[...], -jnp.inf), axis=-1, keepdims=True),
                        jnp.max(s_own, axis=-1, keepdims=True))
        m_ref[...] = m
        p_own = jnp.exp(s_own - m)
        l_ref[...] = jnp.sum(p_own, axis=-1, keepdims=True)
        acc_ref[...] = _dot(_bf16(p_own), v_own)

    @pl.when(g >= ks)
    def _():
        gv = g - ks
        m, bias = m_ref[...], bias_ref[gv]
        den = jnp.zeros((rows, blk), f32)
        o = acc_ref[...]
        for b in range(blks_per_step):
            p_blk = jnp.exp(s_ref[gv, :, b * blk:(b + 1) * blk] + bias[:, b:b + 1] - m)
            den = den + p_blk
            pb = _bf16(p_blk)
            for x in range(pages_per_blk):
                o = o + _dot_nt(pb[:, x * page:(x + 1) * page], _bf16(v_pages[b * pages_per_blk + x][...]))
        acc_ref[...] = o
        l_ref[...] = l_ref[...] + jnp.sum(den, axis=-1, keepdims=True)

    @pl.when(g == 2 * ks - 1)
    def _():
        o64 = jnp.where(own_head, acc_ref[...] / l_ref[...], 0.0)
        o8 = o64[0:t_new]
        for h in range(1, N_HEADS):
            o8 = o8 + o64[h * t_new:(h + 1) * t_new]
        o_ref[0] = _bf16(o8)


def _moba_sample(qb, k_new, v_new, cache_k, cache_v, page_table, n_seq, t_new):
    n_phys, page = cache_k.shape[0], cache_k.shape[1]
    ck = cache_k.transpose(0, 2, 3, 1).reshape(n_phys, ATTN_WIDTH, page)
    cv = cache_v.transpose(0, 2, 3, 1).reshape(n_phys, ATTN_WIDTH, page)
    n_pages = page_table.shape[1]
    pps = PAGES_PER_STEP
    ks = n_pages // pps
    rows = N_HEADS * t_new
    assert n_pages * page // MOBA_BLOCK <= LANES
    assert PAGE_LOOKAHEAD < PAGE_SLOTS and PAGE_LOOKAHEAD <= 2 * ks
    seq_spec = pl.BlockSpec((1, t_new, ATTN_WIDTH), lambda s, g, pt: (s, 0, 0))
    hbm_spec = pl.BlockSpec(memory_space=pl.ANY)

    f32, bf = jnp.float32, jnp.bfloat16
    out = pl.pallas_call(
        _moba_sample_kernel,
        grid_spec=pltpu.PrefetchScalarGridSpec(
            num_scalar_prefetch=1,
            grid=(n_seq, 2 * ks),
            in_specs=[seq_spec, seq_spec, seq_spec, hbm_spec, hbm_spec],
            out_specs=seq_spec,
            scratch_shapes=[pltpu.VMEM((PAGE_SLOTS, pps, ATTN_WIDTH, page), f32),
                            pltpu.SemaphoreType.DMA((PAGE_SLOTS,)),
                            pltpu.VMEM((rows, ATTN_WIDTH), bf),
                            pltpu.VMEM((ks, rows, pps * page), f32),
                            pltpu.VMEM((ks, rows, LANES), f32),
                            pltpu.VMEM((rows, LANES), f32),
                            pltpu.VMEM((rows, LANES), f32),
                            pltpu.VMEM((rows, 1), f32),
                            pltpu.VMEM((rows, 1), f32),
                            pltpu.VMEM((rows, ATTN_WIDTH), f32)]),
        out_shape=jax.ShapeDtypeStruct((n_seq, t_new, ATTN_WIDTH), bf),
        compiler_params=pltpu.CompilerParams(
            dimension_semantics=("arbitrary", "arbitrary"), vmem_limit_bytes=VMEM_LIMIT),
        name="moba_sample",
    )(page_table, qb.reshape(n_seq, t_new, ATTN_WIDTH), k_new.reshape(n_seq, t_new, ATTN_WIDTH),
      v_new.reshape(n_seq, t_new, ATTN_WIDTH), ck, cv)
    return out.reshape(n_seq * t_new, ATTN_WIDTH)


def _post_kernel(x_ref, at_ref, cg_ref, st_ref, wo_ref, g1_ref, b1_ref, wup_ref, fw_ref, fb_ref, wdn_ref,
                 g2_ref, b2_ref, y_ref, tail_ref, carry_ref, hid_ref, *, long_seq, alpha):
    rows = x_ref.shape[0]
    d_ff = wdn_ref.shape[0]
    mix = _dot(jnp.concatenate([at_ref[...], cg_ref[...]], axis=1), wo_ref[...])
    x1 = _layer_norm(alpha * x_ref[...] + mix, g1_ref[...], b1_ref[...])
    x1b = _bf16(x1)

    if long_seq:
        @pl.when(pl.program_id(1) == 0)
        def _():
            carry_ref[...] = st_ref[0]

    for c0 in range(0, d_ff, FF_CHUNK):
        cols = slice(c0, c0 + FF_CHUNK)
        up_a = _dot(x1b, wup_ref[:, cols])
        up_g = _dot(x1b, wup_ref[:, d_ff + c0:d_ff + c0 + FF_CHUNK])
        a3 = up_a.reshape(rows // SUBLANES, SUBLANES, FF_CHUNK)
        if long_seq:
            prev3 = _prev_groups(a3, carry_ref[:, cols])
            carry_ref[:, cols] = a3[-1]
            tail_ref[0, :, cols] = a3[-1]
        else:
            prev3 = st_ref[:, :, cols]
            tail_ref[:, :, cols] = a3
        a_c = (_causal_conv3(a3, prev3, fw_ref, cols) + fb_ref[0:1, cols]).reshape(rows, FF_CHUNK)
        act = 0.5 * a_c * (1.0 + lax.erf(a_c * np.float32(np.sqrt(0.5))))
        hid_ref[:, cols] = _bf16(act * up_g)
    f = _dot(hid_ref[...], wdn_ref[...])
    y_ref[...] = _layer_norm(alpha * x1 + f, g2_ref[...], b2_ref[...])


def _post(x2d, attn, cg, state8, w_o_b, ln1_g, ln1_b, w_up_b, fcw, fcb, w_dn_b, ln2_g, ln2_b,
          *, nb, nt, long_seq, alpha):
    rows, d = x2d.shape
    tm = rows // (nb * nt)
    d_ff = w_dn_b.shape[0]
    row_blk = lambda b, t: (b * nt + t, 0)
    if long_seq:
        st_spec = pl.BlockSpec((1, SUBLANES, d_ff), lambda b, t: (b, 0, 0))
    else:
        st_spec = pl.BlockSpec((tm // SUBLANES, SUBLANES, d_ff), lambda b, t: (t, 0, 0))
    vec = lambda v: v.reshape(1, -1)
    ln1_g, ln1_b, fcb, ln2_g, ln2_b = (vec(v) for v in (ln1_g, ln1_b, fcb, ln2_g, ln2_b))
    consts = (w_o_b, ln1_g, ln1_b, w_up_b, fcw, fcb, w_dn_b, ln2_g, ln2_b)
    return pl.pallas_call(
        functools.partial(_post_kernel, long_seq=long_seq, alpha=alpha),
        grid=(nb, nt),
        in_specs=[pl.BlockSpec((tm, d), row_blk),
                  pl.BlockSpec((tm, attn.shape[1]), row_blk),
                  pl.BlockSpec((tm, cg.shape[1]), row_blk),
                  st_spec] + [_resident(w) for w in consts],
        out_specs=[pl.BlockSpec((tm, d), row_blk), st_spec],
        out_shape=(jax.ShapeDtypeStruct((rows, d), jnp.float32),
                   jax.ShapeDtypeStruct(state8.shape, jnp.float32)),
        scratch_shapes=[pltpu.VMEM((SUBLANES, d_ff), jnp.float32),
                        pltpu.VMEM((tm, d_ff), jnp.bfloat16)],
        compiler_params=pltpu.CompilerParams(
            dimension_semantics=("arbitrary", "arbitrary"), vmem_limit_bytes=VMEM_LIMIT),
        name="post_long" if long_seq else "post_short",
    )(x2d, attn, cg, state8, *consts)


def _pad_state(state):
    return jnp.pad(state, ((0, 0), (SUBLANES - (CONV_K - 1), 0), (0, 0)))


def kernel(x_prompt, x_sample, cache_k, cache_v, page_table, state_conv, state_ffn_conv, w_in, conv_w, w_o,
           ln1_g, ln1_b, w_up, ffn_conv_w, ffn_conv_b, w_down, ln2_g, ln2_b):
    depth = w_in.shape[0]
    bsz, seq, d_model = x_prompt.shape
    n_seq, t_new, _ = x_sample.shape
    conv_width = conv_w.shape[2]
    d_ff = w_down.shape[1]
    past_len = page_table.shape[1] * cache_k.shape[2]
    alpha = (2.0 * depth) ** 0.25
    a = ATTN_WIDTH
    assert t_new == SUBLANES and seq % ROW_TILE == 0 and (n_seq * t_new) % ROW_TILE_SHORT == 0
    assert past_len % MOBA_BLOCK == 0 and past_len // MOBA_BLOCK >= MOBA_TOPK and d_ff % FF_CHUNK == 0
    assert page_table.shape[1] % PAGES_PER_STEP == 0 and (PAGES_PER_STEP * cache_k.shape[2]) % MOBA_BLOCK == 0

    cos_p, sin_p = _rope_angles(jnp.arange(seq, dtype=jnp.int32))
    cos_s, sin_s = _rope_angles(past_len + jnp.arange(t_new, dtype=jnp.int32))
    lane_tabs_p = _rope_lane_tables(cos_p, sin_p)
    row_tabs_p = (cos_p.T, sin_p.T)
    lane_tabs_s = tuple(jnp.tile(t, (ROW_TILE_SHORT // t_new, 1)) for t in _rope_lane_tables(cos_s, sin_s))
    grid_p = dict(nb=bsz, nt=seq // ROW_TILE, long_seq=True)
    grid_s = dict(nb=1, nt=n_seq * t_new // ROW_TILE_SHORT, long_seq=False)
    zeros_conv = jnp.zeros((bsz, SUBLANES, conv_width), jnp.float32)
    zeros_ffn = jnp.zeros((bsz, SUBLANES, d_ff), jnp.float32)

    xp = x_prompt.reshape(bsz * seq, d_model)
    xs = x_sample.reshape(n_seq * t_new, d_model)
    outs = [[] for _ in range(8)]
    for l in range(depth):
        w_in_b, w_o_b, w_up_b, w_dn_b = _bf16(w_in[l]), _bf16(w_o[l]), _bf16(w_up[l]), _bf16(w_down[l])
        w_qkvt_b = w_in_b[:, 0:3 * a].T
        post_w = (w_o_b, ln1_g[l], ln1_b[l], w_up_b, ffn_conv_w[l], ffn_conv_b[l], w_dn_b, ln2_g[l], ln2_b[l])

        qt, k_rm, kt, vt, vtb, cg, tail = _proj_long(
            xp, w_in_b, w_qkvt_b, lane_tabs_p, row_tabs_p, conv_w[l], zeros_conv, bsz, seq)
        attn = _moba_prompt(qt, k_rm, vtb, bsz, seq)
        xp, ffn_tail = _post(xp, attn, cg, zeros_ffn, *post_w, alpha=alpha, **grid_p)
        new = (kt.reshape(bsz, N_HEADS, HEAD_DIM, seq).transpose(0, 3, 1, 2),
               vt.reshape(bsz, N_HEADS, HEAD_DIM, seq).transpose(0, 3, 1, 2),
               tail[:, -(CONV_K - 1):], ffn_tail[:, -(CONV_K - 1):])
        for dst, val in zip(outs[0::2], new):
            dst.append(val)

        qb, k, v, cg, tail = _proj_short(xs, w_in_b, lane_tabs_s, conv_w[l], _pad_state(state_conv[l]))
        attn = _moba_sample(qb, k, v, cache_k[l], cache_v[l], page_table, n_seq, t_new)
        xs, ffn_tail = _post(xs, attn, cg, _pad_state(state_ffn_conv[l]), *post_w, alpha=alpha, **grid_s)
        new = (k.reshape(n_seq, t_new, N_HEADS, HEAD_DIM), v.reshape(n_seq, t_new, N_HEADS, HEAD_DIM),
               tail[:, -(CONV_K - 1):], ffn_tail[:, -(CONV_K - 1):])
        for dst, val in zip(outs[1::2], new):
            dst.append(val)

    kp, ks, vp, vs, cp, cs, fp, fs = (jnp.stack(o) for o in outs)
    return (xp.reshape(bsz, seq, d_model), xs.reshape(n_seq, t_new, d_model), kp, vp, ks, vs, cp, cs, fp, fs)
```

```python
import functools

import jax
import jax.numpy as jnp
import numpy as np
from jax import lax
from jax.experimental import pallas as pl
from jax.experimental.pallas import tpu as pltpu

N_HEADS = 8
HEAD_DIM = 64
ATTN_WIDTH = N_HEADS * HEAD_DIM
ROT_DIM = HEAD_DIM // 4
ROPE_THETA = 500000.0
MOBA_BLOCK = 256
MOBA_TOPK = 3
CONV_K = 3
LN_EPS = 1e-5

LANES = 128
SUBLANES = 8
BF16_ROWS = 16
ROW_TILE = 512
ROW_TILE_SHORT = 256
FF_CHUNK = 256
PAGES_PER_STEP = 32
PAGE_SLOTS = 3
PAGE_LOOKAHEAD = 2
PAST_BLOCKS_PER_TRIP = 8
HEAD_TILES_PER_STEP = 4
VMEM_LIMIT = 56 * 1024 * 1024
MASK_BIAS = -1e30

_NT = (((1,), (1,)), ((), ()))


def _bf16(x):
    return x.astype(jnp.bfloat16)


def _dot(a, b):
    return jnp.dot(a, b, preferred_element_type=jnp.float32)


def _dot_nt(a, b):
    return lax.dot_general(a, b, _NT, preferred_element_type=jnp.float32)


def _layer_norm(x, g, b):
    mu = jnp.mean(x, axis=-1, keepdims=True)
    xc = x - mu
    var = jnp.mean(xc * xc, axis=-1, keepdims=True)
    return xc * lax.rsqrt(var + LN_EPS) * g + b


def _resident(arr):
    return pl.BlockSpec(arr.shape, lambda *_: (0,) * arr.ndim, pipeline_mode=pl.Buffered(1))


def _causal_conv3(u3, prev3, w_ref, cols):
    sub = lax.broadcasted_iota(jnp.int32, u3.shape, 1)
    s1 = jnp.where(sub < 1, pltpu.roll(prev3, 1, 1), pltpu.roll(u3, 1, 1))
    s2 = jnp.where(sub < 2, pltpu.roll(prev3, 2, 1), pltpu.roll(u3, 2, 1))
    return s2 * w_ref[0:1, cols] + s1 * w_ref[1:2, cols] + u3 * w_ref[2:3, cols]


def _prev_groups(u3, carry):
    return jnp.concatenate([carry[None], u3[:-1]], axis=0)


def _rope_lanes(z, ra_ref, rb_ref, rc_ref):
    outs = []
    for c in range(z.shape[1] // LANES):
        zc = z[:, c * LANES:(c + 1) * LANES]
        outs.append(zc * ra_ref[...]
                    + pltpu.roll(zc, LANES - ROT_DIM // 2, 1) * rb_ref[...]
                    + pltpu.roll(zc, ROT_DIM // 2, 1) * rc_ref[...])
    return jnp.concatenate(outs, axis=1)


def _rope_rows(zt, cos_ref, sin_ref):
    half = ROT_DIM // 2
    cos, sin = cos_ref[...], sin_ref[...]
    parts = []
    for h in range(N_HEADS):
        r0 = h * HEAD_DIM
        x1, x2 = zt[r0:r0 + half], zt[r0 + half:r0 + ROT_DIM]
        parts += [x1 * cos - x2 * sin, x2 * cos + x1 * sin, zt[r0 + ROT_DIM:r0 + HEAD_DIM]]
    return jnp.concatenate(parts, axis=0)


def _gated_conv(xb, w_ref, cw_ref, prev3_fn, cg_ref):
    a = ATTN_WIDTH
    c = cw_ref.shape[1]
    rows = xb.shape[0]
    gb = _dot(xb, w_ref[:, 3 * a:3 * a + c])
    u = _dot(xb, w_ref[:, 3 * a + c:3 * a + 2 * c]) * _dot(xb, w_ref[:, 3 * a + 2 * c:3 * a + 3 * c])
    u3 = u.reshape(rows // SUBLANES, SUBLANES, c)
    y = _causal_conv3(u3, prev3_fn(u3), cw_ref, slice(None)).reshape(rows, c)
    cg_ref[...] = _bf16(gb * y)
    return u3


def _proj_long_kernel(x_ref, w_ref, wt_ref, ra_ref, rb_ref, rc_ref, cos_ref, sin_ref, cw_ref, st_ref,
                      qt_ref, k_ref, kt_ref, vt_ref, vtb_ref, cg_ref, tail_ref, carry_ref):
    a = ATTN_WIDTH
    xb = _bf16(x_ref[...])
    qkvt = _dot_nt(wt_ref[...], xb)
    qt_ref[0] = _bf16(_rope_rows(qkvt[0:a], cos_ref, sin_ref) * (HEAD_DIM ** -0.5))
    kt_ref[0] = _rope_rows(qkvt[a:2 * a], cos_ref, sin_ref)
    vt = qkvt[2 * a:3 * a]
    vt_ref[0] = vt
    for c in range(vtb_ref.shape[1]):
        vtb_ref[0, c] = _bf16(vt[:, c * MOBA_BLOCK:(c + 1) * MOBA_BLOCK])
    k_ref[...] = _bf16(_rope_lanes(_dot(xb, w_ref[:, a:2 * a]), ra_ref, rb_ref, rc_ref))

    @pl.when(pl.program_id(1) == 0)
    def _():
        carry_ref[...] = st_ref[0]

    u3 = _gated_conv(xb, w_ref, cw_ref, lambda u3: _prev_groups(u3, carry_ref[...]), cg_ref)
    carry_ref[...] = u3[-1]
    tail_ref[0] = u3[-1]


def _proj_short_kernel(x_ref, w_ref, ra_ref, rb_ref, rc_ref, cw_ref, st_ref,
                       q_ref, k_ref, v_ref, cg_ref, tail_ref):
    a = ATTN_WIDTH
    xb = _bf16(x_ref[...])
    q = _rope_lanes(_dot(xb, w_ref[:, 0:a]), ra_ref, rb_ref, rc_ref)
    q_ref[...] = _bf16(q * (HEAD_DIM ** -0.5))
    k_ref[...] = _rope_lanes(_dot(xb, w_ref[:, a:2 * a]), ra_ref, rb_ref, rc_ref)
    v_ref[...] = _dot(xb, w_ref[:, 2 * a:3 * a])
    tail_ref[...] = _gated_conv(xb, w_ref, cw_ref, lambda u3: st_ref[...], cg_ref)


def _proj_long(x2d, w_in_b, w_qkvt_b, lane_tabs, row_tabs, conv_w, state8, bsz, seq):
    rows, d = x2d.shape
    tm = ROW_TILE
    nt = seq // tm
    a, c = ATTN_WIDTH, conv_w.shape[1]
    nblk = tm // MOBA_BLOCK
    row_blk = lambda b, t: (b * nt + t, 0)
    tab_spec = pl.BlockSpec((tm, LANES), lambda b, t: (t, 0))
    rtab_spec = pl.BlockSpec((ROT_DIM // 2, tm), lambda b, t: (0, t))
    st_spec = pl.BlockSpec((1, SUBLANES, c), lambda b, t: (b, 0, 0))
    kt_spec = pl.BlockSpec((1, a, tm), lambda b, t: (b, 0, t))
    vtb_spec = pl.BlockSpec((1, nblk, a, MOBA_BLOCK), lambda b, t: (b, t, 0, 0))
    f32, bf = jnp.float32, jnp.bfloat16
    out_shape = (
        jax.ShapeDtypeStruct((bsz, a, seq), bf),
        jax.ShapeDtypeStruct((rows, a), bf),
        jax.ShapeDtypeStruct((bsz, a, seq), f32), jax.ShapeDtypeStruct((bsz, a, seq), f32),
        jax.ShapeDtypeStruct((bsz, seq // MOBA_BLOCK, a, MOBA_BLOCK), bf),
        jax.ShapeDtypeStruct((rows, c), bf),
        jax.ShapeDtypeStruct(state8.shape, f32),
    )
    return pl.pallas_call(
        _proj_long_kernel,
        grid=(bsz, nt),
        in_specs=[pl.BlockSpec((tm, d), row_blk), _resident(w_in_b), _resident(w_qkvt_b),
                  tab_spec, tab_spec, tab_spec, rtab_spec, rtab_spec, _resident(conv_w), st_spec],
        out_specs=[kt_spec, pl.BlockSpec((tm, a), row_blk), kt_spec, kt_spec, vtb_spec,
                   pl.BlockSpec((tm, c), row_blk), st_spec],
        out_shape=out_shape,
        scratch_shapes=[pltpu.VMEM((SUBLANES, c), f32)],
        compiler_params=pltpu.CompilerParams(
            dimension_semantics=("arbitrary", "arbitrary"), vmem_limit_bytes=VMEM_LIMIT),
        name="proj_long",
    )(x2d, w_in_b, w_qkvt_b, *lane_tabs, *row_tabs, conv_w, state8)


def _proj_short(x2d, w_in_b, lane_tabs, conv_w, state8):
    rows, d = x2d.shape
    tm = ROW_TILE_SHORT
    a, c = ATTN_WIDTH, conv_w.shape[1]
    row_blk = lambda t: (t, 0)
    tab_spec = pl.BlockSpec((tm, LANES), lambda t: (0, 0))
    st_spec = pl.BlockSpec((tm // SUBLANES, SUBLANES, c), lambda t: (t, 0, 0))
    f32, bf = jnp.float32, jnp.bfloat16
    out_shape = (
        jax.ShapeDtypeStruct((rows, a), bf), jax.ShapeDtypeStruct((rows, a), f32),
        jax.ShapeDtypeStruct((rows, a), f32), jax.ShapeDtypeStruct((rows, c), bf),
        jax.ShapeDtypeStruct(state8.shape, f32),
    )
    return pl.pallas_call(
        _proj_short_kernel,
        grid=(rows // tm,),
        in_specs=[pl.BlockSpec((tm, d), row_blk), _resident(w_in_b),
                  tab_spec, tab_spec, tab_spec, _resident(conv_w), st_spec],
        out_specs=[pl.BlockSpec((tm, a), row_blk)] * 3 + [pl.BlockSpec((tm, c), row_blk), st_spec],
        out_shape=out_shape,
        compiler_params=pltpu.CompilerParams(
            dimension_semantics=("arbitrary",), vmem_limit_bytes=VMEM_LIMIT),
        name="proj_short",
    )(x2d, w_in_b, *lane_tabs, conv_w, state8)


def _rope_angles(pos):
    half = ROT_DIM // 2
    inv = 1.0 / (ROPE_THETA ** (jnp.arange(half, dtype=jnp.float32) / half))
    ang = pos.astype(jnp.float32)[:, None] * inv[None, :]
    return jnp.cos(ang), jnp.sin(ang)


def _rope_lane_tables(cos, sin):
    t, half = cos.shape
    rest = HEAD_DIM - ROT_DIM
    ones, zeros = jnp.ones((t, rest), jnp.float32), jnp.zeros((t, rest), jnp.float32)
    zh = jnp.zeros((t, half), jnp.float32)
    ta = jnp.concatenate([cos, cos, ones], axis=1)
    tb = jnp.concatenate([-sin, zh, zeros], axis=1)
    tc = jnp.concatenate([zh, sin, zeros], axis=1)
    return tuple(jnp.tile(x, (1, LANES // HEAD_DIM)) for x in (ta, tb, tc))


def _top_mask(gate, n_valid, axis):
    idx = lax.broadcasted_iota(jnp.int32, gate.shape, axis)
    n = gate.shape[axis]
    g = jnp.where(idx < n_valid, gate, -jnp.inf)
    sel = jnp.zeros(gate.shape, jnp.bool_)
    for _ in range(MOBA_TOPK):
        mx = jnp.max(g, axis=axis, keepdims=True)
        first = jnp.min(jnp.where(g == mx, idx, n), axis=axis, keepdims=True)
        pick = (idx == first) & (mx > -jnp.inf)
        sel = sel | pick
        g = jnp.where(pick, -jnp.inf, g)
    return sel


def _moba_prompt_kernel(qt_ref, k_ref, vt_ref, o_ref, km_ref, m_ref, acc_ref):
    i = pl.program_id(2)
    blk = MOBA_BLOCK
    nb = vt_ref.shape[1]
    tiles = km_ref.shape[0]
    nbp = -(-nb // BF16_ROWS) * BF16_ROWS
    f32, bf = jnp.float32, jnp.bfloat16

    @pl.when(i == 0)
    def _():
        for ti in range(tiles):
            kf = k_ref[0, :, ti * LANES:(ti + 1) * LANES].astype(f32).reshape(nb, blk, LANES)
            km_ref[ti] = jnp.zeros((LANES, LANES), f32)
            km_ref[ti, 0:nb, :] = jnp.sum(kf, axis=1) * (1.0 / blk)

    row_q = lax.broadcasted_iota(jnp.int32, (LANES, blk), 0)
    head0 = _bf16(jnp.where(row_q < HEAD_DIM, 1.0, 0.0))
    head1 = _bf16(jnp.where(row_q < HEAD_DIM, 0.0, 1.0))
    lane_k = lax.broadcasted_iota(jnp.int32, (blk, LANES), 1)
    ones_rows = jnp.ones((BF16_ROWS, blk), bf)

    def make_block_fn(ti):
        lanes = slice(ti * LANES, (ti + 1) * LANES)
        qt = qt_ref[0, lanes, :]
        q2 = jnp.concatenate([qt * head0, qt * head1], axis=1)
        gate = _dot(km_ref[ti], q2.astype(f32))[0:nbp]
        row_g = lax.broadcasted_iota(jnp.int32, gate.shape, 0)
        visible = _top_mask(gate, i, 0) | (row_g == i)
        bias = _bf16(jnp.where(visible, 0.0, MASK_BIAS))
        q_aug = jnp.concatenate([q2, bias, jnp.zeros((LANES - nbp, 2 * blk), bf)], axis=0)

        def block(j):
            start = pl.multiple_of(j * blk, blk)
            k_aug = jnp.concatenate([k_ref[0, pl.ds(start, blk), lanes],
                                     _bf16(jnp.where(lane_k == j, 1.0, 0.0))], axis=1)
            st = _dot(k_aug, q_aug)
            vv = jnp.concatenate([vt_ref[0, j, lanes, :], ones_rows], axis=0)
            return st, vv

        return block

    block_fns = [make_block_fn(ti) for ti in range(tiles)]

    def fold(state, sts, vvs):
        m_new = functools.reduce(jnp.maximum, [jnp.max(st, axis=0, keepdims=True) for st in sts])
        if state is not None:
            m_new = jnp.maximum(state[0], m_new)
        pv = functools.reduce(jnp.add, [_dot(vv, _bf16(jnp.exp(st - m_new))) for st, vv in zip(sts, vvs)])
        if state is not None:
            pv = state[1] * jnp.exp(state[0] - m_new) + pv
        return m_new, pv

    per_trip = PAST_BLOCKS_PER_TRIP
    n_tail = i % per_trip
    for k in range(per_trip):
        @pl.when(n_tail == k)
        def _(k=k):
            for ti, block in enumerate(block_fns):
                st, vv = block(i)
                key = lax.broadcasted_iota(jnp.int32, st.shape, 0)
                col = lax.broadcasted_iota(jnp.int32, st.shape, 1)
                qpos = jnp.where(col >= blk, col - blk, col)
                sts, vvs = [jnp.where(key <= qpos, st, -jnp.inf)], [vv]
                for u in range(k):
                    st, vv = block(i - k + u)
                    sts.append(st)
                    vvs.append(vv)
                m_ref[ti], acc_ref[ti] = fold(None, sts, vvs)

    def body(t, carry):
        for ti, block in enumerate(block_fns):
            sts, vvs = zip(*[block(per_trip * t + u) for u in range(per_trip)])
            m_ref[ti], acc_ref[ti] = fold((m_ref[ti], acc_ref[ti]), sts, vvs)
        return carry

    lax.fori_loop(0, i // per_trip, body, 0)

    for ti in range(tiles):
        acc = acc_ref[ti]
        ot = acc[0:LANES] / acc[LANES:LANES + 1]
        x = jnp.concatenate([ot[0:HEAD_DIM, 0:blk], ot[HEAD_DIM:LANES, blk:2 * blk]], axis=0)
        o_ref[0, :, ti * LANES:(ti + 1) * LANES] = _bf16(x.T)


def _moba_prompt(qt, k_rm, vtb, bsz, seq):
    blk = MOBA_BLOCK
    nb = seq // blk
    tiles = HEAD_TILES_PER_STEP
    width = tiles * LANES
    assert nb <= LANES and ATTN_WIDTH % width == 0
    k3 = k_rm.reshape(bsz, seq, ATTN_WIDTH)
    qt_spec = pl.BlockSpec((1, width, blk), lambda b, h, i: (b, h, i))
    k_spec = pl.BlockSpec((1, seq, width), lambda b, h, i: (b, 0, h))
    vt_spec = pl.BlockSpec((1, nb, width, blk), lambda b, h, i: (b, 0, h, 0))
    out = pl.pallas_call(
        _moba_prompt_kernel,
        grid=(bsz, ATTN_WIDTH // width, nb),
        in_specs=[qt_spec, k_spec, vt_spec],
        out_specs=pl.BlockSpec((1, blk, width), lambda b, h, i: (b, i, h)),
        out_shape=jax.ShapeDtypeStruct((bsz, seq, ATTN_WIDTH), jnp.bfloat16),
        scratch_shapes=[pltpu.VMEM((tiles, LANES, LANES), jnp.float32),
                        pltpu.VMEM((tiles, 1, 2 * blk), jnp.float32),
                        pltpu.VMEM((tiles, LANES + BF16_ROWS, 2 * blk), jnp.float32)],
        compiler_params=pltpu.CompilerParams(
            dimension_semantics=("arbitrary", "arbitrary", "arbitrary"), vmem_limit_bytes=VMEM_LIMIT),
        name="moba_prompt",
    )(qt, k3, vtb)
    return out.reshape(bsz * seq, ATTN_WIDTH)


def _moba_sample_kernel(pt_ref, q_ref, k1_ref, v1_ref, ck_ref, cv_ref, o_ref,
                        buf_ref, sem_ref, wq_ref, s_ref, bias_ref, gate_ref, bmax_ref, m_ref, l_ref, acc_ref):
    seq, g = pl.program_id(0), pl.program_id(1)
    steps = pl.num_programs(1)
    ks = s_ref.shape[0]
    n_slot, n_pg = buf_ref.shape[0], buf_ref.shape[1]
    step = seq * steps + g
    total_steps = pl.num_programs(0) * steps

    def page_copy(src_ref, page_id, slot, p):
        return pltpu.make_async_copy(src_ref.at[page_id], buf_ref.at[slot, p], sem_ref.at[slot])

    def fetch(ahead):
        g_raw = g + ahead
        wrap = (g_raw >= steps).astype(jnp.int32)
        seq_f, g_f = seq + wrap, g_raw - wrap * steps
        slot = (step + ahead) % n_slot

        @pl.when(g_f < ks)
        def _():
            for p in range(n_pg):
                page_copy(ck_ref, pt_ref[seq_f, g_f * n_pg + p], slot, p).start()

        @pl.when(g_f >= ks)
        def _():
            for p in range(n_pg):
                page_copy(cv_ref, pt_ref[seq_f, (g_f - ks) * n_pg + p], slot, p).start()

    @pl.when(step == 0)
    def _():
        for ahead in range(PAGE_LOOKAHEAD):
            fetch(ahead)

    @pl.when(step + PAGE_LOOKAHEAD < total_steps)
    def _():
        fetch(PAGE_LOOKAHEAD)

    slot = step % n_slot
    for p in range(n_pg):
        page_copy(ck_ref, 0, slot, p).wait()
    k_pages = v_pages = [buf_ref.at[slot, p] for p in range(n_pg)]

    f32 = jnp.float32
    t_new = q_ref.shape[1]
    rows = N_HEADS * t_new
    page = buf_ref.shape[3]
    blk = MOBA_BLOCK
    pages_per_blk = blk // page
    blks_per_step = n_pg // pages_per_blk
    nblk = ks * blks_per_step
    row = lax.broadcasted_iota(jnp.int32, (rows, ATTN_WIDTH), 0)
    lane = lax.broadcasted_iota(jnp.int32, (rows, ATTN_WIDTH), 1)
    own_head = (row // t_new) == (lane // HEAD_DIM)

    @pl.when(g == 0)
    def _():
        q8 = q_ref[0].astype(f32)
        wq_ref[...] = _bf16(jnp.where(own_head, jnp.concatenate([q8] * N_HEADS, axis=0), 0.0))
        gate_ref[...] = jnp.zeros(gate_ref.shape, f32)
        bmax_ref[...] = jnp.zeros(bmax_ref.shape, f32)

    @pl.when(g < ks)
    def _():
        wq = wq_ref[...]
        lane_g = lax.broadcasted_iota(jnp.int32, (rows, LANES), 1)
        gate, bmax = gate_ref[...], bmax_ref[...]
        for b in range(blks_per_step):
            pg = range(b * pages_per_blk, (b + 1) * pages_per_blk)
            s_blk = jnp.concatenate([_dot(wq, _bf16(k_pages[p][...])) for p in pg], axis=1)
            s_ref[g, :, b * blk:(b + 1) * blk] = s_blk
            hit = lane_g == g * blks_per_step + b
            gate = jnp.where(hit, jnp.sum(s_blk, axis=-1, keepdims=True), gate)
            bmax = jnp.where(hit, jnp.max(s_blk, axis=-1, keepdims=True), bmax)
        gate_ref[...] = gate
        bmax_ref[...] = bmax

    @pl.when(g == ks - 1)
    def _():
        wq = wq_ref[...]
        sel = _top_mask(gate_ref[...], nblk, 1)
        bias = jnp.where(sel, 0.0, -jnp.inf)
        for gi in range(ks):
            shift = (LANES - gi * blks_per_step) % LANES
            bias_ref[gi] = pltpu.roll(bias, shift, 1) if shift else bias
        zpad = jnp.zeros((LANES - t_new, ATTN_WIDTH), jnp.bfloat16)
        k_own = jnp.concatenate([_bf16(k1_ref[0]), zpad], axis=0)
        v_own = jnp.concatenate([_bf16(v1_ref[0]), zpad], axis=0)
        s_own = _dot_nt(wq, k_own)
        r_o = lax.broadcasted_iota(jnp.int32, s_own.shape, 0)
        c_o = lax.broadcasted_iota(jnp.int32, s_own.shape, 1)
        s_own = jnp.where(c_o <= r_o % t_new, s_own, -jnp.inf)
        m = jnp.maximum(jnp.max(jnp.where(sel, bmax_ref[...], -jnp.inf), axis=-1, keepdims=True),
                        jnp.max(s_own, axis=-1, keepdims=True))
        m_ref[...] = m
        p_own = jnp.exp(s_own - m)
        l_ref[...] = jnp.sum(p_own, axis=-1, keepdims=True)
        acc_ref[...] = _dot(_bf16(p_own), v_own)

    @pl.when(g >= ks)
    def _():
        gv = g - ks
        m, bias = m_ref[...], bias_ref[gv]
        den = jnp.zeros((rows, blk), f32)
        o = acc_ref[...]
        for b in range(blks_per_step):
            p_blk = jnp.exp(s_ref[gv, :, b * blk:(b + 1) * blk] + bias[:, b:b + 1] - m)
            den = den + p_blk
            pb = _bf16(p_blk)
            for x in range(pages_per_blk):
                o = o + _dot_nt(pb[:, x * page:(x + 1) * page], _bf16(v_pages[b * pages_per_blk + x][...]))
        acc_ref[...] = o
        l_ref[...] = l_ref[...] + jnp.sum(den, axis=-1, keepdims=True)

    @pl.when(g == 2 * ks - 1)
    def _():
        o64 = jnp.where(own_head, acc_ref[...] / l_ref[...], 0.0)
        o8 = o64[0:t_new]
        for h in range(1, N_HEADS):
            o8 = o8 + o64[h * t_new:(h + 1) * t_new]
        o_ref[0] = _bf16(o8)


def _moba_sample(qb, k_new, v_new, cache_k, cache_v, page_table, n_seq, t_new):
    n_phys, page = cache_k.shape[0], cache_k.shape[1]
    ck = cache_k.transpose(0, 2, 3, 1).reshape(n_phys, ATTN_WIDTH, page)
    cv = cache_v.transpose(0, 2, 3, 1).reshape(n_phys, ATTN_WIDTH, page)
    n_pages = page_table.shape[1]
    pps = PAGES_PER_STEP
    ks = n_pages // pps
    rows = N_HEADS * t_new
    assert n_pages * page // MOBA_BLOCK <= LANES
    assert PAGE_LOOKAHEAD < PAGE_SLOTS and PAGE_LOOKAHEAD <= 2 * ks
    seq_spec = pl.BlockSpec((1, t_new, ATTN_WIDTH), lambda s, g, pt: (s, 0, 0))
    hbm_spec = pl.BlockSpec(memory_space=pl.ANY)

    f32, bf = jnp.float32, jnp.bfloat16
    out = pl.pallas_call(
        _moba_sample_kernel,
        grid_spec=pltpu.PrefetchScalarGridSpec(
            num_scalar_prefetch=1,
            grid=(n_seq, 2 * ks),
            in_specs=[seq_spec, seq_spec, seq_spec, hbm_spec, hbm_spec],
            out_specs=seq_spec,
            scratch_shapes=[pltpu.VMEM((PAGE_SLOTS, pps, ATTN_WIDTH, page), f32),
                            pltpu.SemaphoreType.DMA((PAGE_SLOTS,)),
                            pltpu.VMEM((rows, ATTN_WIDTH), bf),
                            pltpu.VMEM((ks, rows, pps * page), f32),
                            pltpu.VMEM((ks, rows, LANES), f32),
                            pltpu.VMEM((rows, LANES), f32),
                            pltpu.VMEM((rows, LANES), f32),
                            pltpu.VMEM((rows, 1), f32),
                            pltpu.VMEM((rows, 1), f32),
                            pltpu.VMEM((rows, ATTN_WIDTH), f32)]),
        out_shape=jax.ShapeDtypeStruct((n_seq, t_new, ATTN_WIDTH), bf),
        compiler_params=pltpu.CompilerParams(
            dimension_semantics=("arbitrary", "arbitrary"), vmem_limit_bytes=VMEM_LIMIT),
        name="moba_sample",
    )(page_table, qb.reshape(n_seq, t_new, ATTN_WIDTH), k_new.reshape(n_seq, t_new, ATTN_WIDTH),
      v_new.reshape(n_seq, t_new, ATTN_WIDTH), ck, cv)
    return out.reshape(n_seq * t_new, ATTN_WIDTH)


def _post_kernel(x_ref, at_ref, cg_ref, st_ref, wo_ref, g1_ref, b1_ref, wup_ref, fw_ref, fb_ref, wdn_ref,
                 g2_ref, b2_ref, y_ref, tail_ref, carry_ref, hid_ref, *, long_seq, alpha):
    rows = x_ref.shape[0]
    d_ff = wdn_ref.shape[0]
    mix = _dot(jnp.concatenate([at_ref[...], cg_ref[...]], axis=1), wo_ref[...])
    x1 = _layer_norm(alpha * x_ref[...] + mix, g1_ref[...], b1_ref[...])
    x1b = _bf16(x1)

    if long_seq:
        @pl.when(pl.program_id(1) == 0)
        def _():
            carry_ref[...] = st_ref[0]

    for c0 in range(0, d_ff, FF_CHUNK):
        cols = slice(c0, c0 + FF_CHUNK)
        up_a = _dot(x1b, wup_ref[:, cols])
        up_g = _dot(x1b, wup_ref[:, d_ff + c0:d_ff + c0 + FF_CHUNK])
        a3 = up_a.reshape(rows // SUBLANES, SUBLANES, FF_CHUNK)
        if long_seq:
            prev3 = _prev_groups(a3, carry_ref[:, cols])
            carry_ref[:, cols] = a3[-1]
            tail_ref[0, :, cols] = a3[-1]
        else:
            prev3 = st_ref[:, :, cols]
            tail_ref[:, :, cols] = a3
        a_c = (_causal_conv3(a3, prev3, fw_ref, cols) + fb_ref[0:1, cols]).reshape(rows, FF_CHUNK)
        act = 0.5 * a_c * (1.0 + lax.erf(a_c * np.float32(np.sqrt(0.5))))
        hid_ref[:, cols] = _bf16(act * up_g)
    f = _dot(hid_ref[...], wdn_ref[...])
    y_ref[...] = _layer_norm(alpha * x1 + f, g2_ref[...], b2_ref[...])


def _post(x2d, attn, cg, state8, w_o_b, ln1_g, ln1_b, w_up_b, fcw, fcb, w_dn_b, ln2_g, ln2_b,
          *, nb, nt, long_seq, alpha):
    rows, d = x2d.shape
    tm = rows // (nb * nt)
    d_ff = w_dn_b.shape[0]
    row_blk = lambda b, t: (b * nt + t, 0)
    if long_seq:
        st_spec = pl.BlockSpec((1, SUBLANES, d_ff), lambda b, t: (b, 0, 0))
    else:
        st_spec = pl.BlockSpec((tm // SUBLANES, SUBLANES, d_ff), lambda b, t: (t, 0, 0))
    vec = lambda v: v.reshape(1, -1)
    ln1_g, ln1_b, fcb, ln2_g, ln2_b = (vec(v) for v in (ln1_g, ln1_b, fcb, ln2_g, ln2_b))
    consts = (w_o_b, ln1_g, ln1_b, w_up_b, fcw, fcb, w_dn_b, ln2_g, ln2_b)
    return pl.pallas_call(
        functools.partial(_post_kernel, long_seq=long_seq, alpha=alpha),
        grid=(nb, nt),
        in_specs=[pl.BlockSpec((tm, d), row_blk),
                  pl.BlockSpec((tm, attn.shape[1]), row_blk),
                  pl.BlockSpec((tm, cg.shape[1]), row_blk),
                  st_spec] + [_resident(w) for w in consts],
        out_specs=[pl.BlockSpec((tm, d), row_blk), st_spec],
        out_shape=(jax.ShapeDtypeStruct((rows, d), jnp.float32),
                   jax.ShapeDtypeStruct(state8.shape, jnp.float32)),
        scratch_shapes=[pltpu.VMEM((SUBLANES, d_ff), jnp.float32),
                        pltpu.VMEM((tm, d_ff), jnp.bfloat16)],
        compiler_params=pltpu.CompilerParams(
            dimension_semantics=("arbitrary", "arbitrary"), vmem_limit_bytes=VMEM_LIMIT),
        name="post_long" if long_seq else "post_short",
    )(x2d, attn, cg, state8, *consts)


def _pad_state(state):
    return jnp.pad(state, ((0, 0), (SUBLANES - (CONV_K - 1), 0), (0, 0)))


def kernel(x_prompt, x_sample, cache_k, cache_v, page_table, state_conv, state_ffn_conv, w_in, conv_w, w_o,
           ln1_g, ln1_b, w_up, ffn_conv_w, ffn_conv_b, w_down, ln2_g, ln2_b):
    depth = w_in.shape[0]
    bsz, seq, d_model = x_prompt.shape
    n_seq, t_new, _ = x_sample.shape
    conv_width = conv_w.shape[2]
    d_ff = w_down.shape[1]
    past_len = page_table.shape[1] * cache_k.shape[2]
    alpha = (2.0 * depth) ** 0.25
    a = ATTN_WIDTH
    assert t_new == SUBLANES and seq % ROW_TILE == 0 and (n_seq * t_new) % ROW_TILE_SHORT == 0
    assert past_len % MOBA_BLOCK == 0 and past_len // MOBA_BLOCK >= MOBA_TOPK and d_ff % FF_CHUNK == 0
    assert page_table.shape[1] % PAGES_PER_STEP == 0 and (PAGES_PER_STEP * cache_k.shape[2]) % MOBA_BLOCK == 0

    cos_p, sin_p = _rope_angles(jnp.arange(seq, dtype=jnp.int32))
    cos_s, sin_s = _rope_angles(past_len + jnp.arange(t_new, dtype=jnp.int32))
    lane_tabs_p = _rope_lane_tables(cos_p, sin_p)
    row_tabs_p = (cos_p.T, sin_p.T)
    lane_tabs_s = tuple(jnp.tile(t, (ROW_TILE_SHORT // t_new, 1)) for t in _rope_lane_tables(cos_s, sin_s))
    grid_p = dict(nb=bsz, nt=seq // ROW_TILE, long_seq=True)
    grid_s = dict(nb=1, nt=n_seq * t_new // ROW_TILE_SHORT, long_seq=False)
    zeros_conv = jnp.zeros((bsz, SUBLANES, conv_width), jnp.float32)
    zeros_ffn = jnp.zeros((bsz, SUBLANES, d_ff), jnp.float32)

    xp = x_prompt.reshape(bsz * seq, d_model)
    xs = x_sample.reshape(n_seq * t_new, d_model)
    outs = [[] for _ in range(8)]
    for l in range(depth):
        w_in_b, w_o_b, w_up_b, w_dn_b = _bf16(w_in[l]), _bf16(w_o[l]), _bf16(w_up[l]), _bf16(w_down[l])
        w_qkvt_b = w_in_b[:, 0:3 * a].T
        post_w = (w_o_b, ln1_g[l], ln1_b[l], w_up_b, ffn_conv_w[l], ffn_conv_b[l], w_dn_b, ln2_g[l], ln2_b[l])

        qt, k_rm, kt, vt, vtb, cg, tail = _proj_long(
            xp, w_in_b, w_qkvt_b, lane_tabs_p, row_tabs_p, conv_w[l], zeros_conv, bsz, seq)
        attn = _moba_prompt(qt, k_rm, vtb, bsz, seq)
        xp, ffn_tail = _post(xp, attn, cg, zeros_ffn, *post_w, alpha=alpha, **grid_p)
        new = (kt.reshape(bsz, N_HEADS, HEAD_DIM, seq).transpose(0, 3, 1, 2),
               vt.reshape(bsz, N_HEADS, HEAD_DIM, seq).transpose(0, 3, 1, 2),
               tail[:, -(CONV_K - 1):], ffn_tail[:, -(CONV_K - 1):])
        for dst, val in zip(outs[0::2], new):
            dst.append(val)

        qb, k, v, cg, tail = _proj_short(xs, w_in_b, lane_tabs_s, conv_w[l], _pad_state(state_conv[l]))
        attn = _moba_sample(qb, k, v, cache_k[l], cache_v[l], page_table, n_seq, t_new)
        xs, ffn_tail = _post(xs, attn, cg, _pad_state(state_ffn_conv[l]), *post_w, alpha=alpha, **grid_s)
        new = (k.reshape(n_seq, t_new, N_HEADS, HEAD_DIM), v.reshape(n_seq, t_new, N_HEADS, HEAD_DIM),
               tail[:, -(CONV_K - 1):], ffn_tail[:, -(CONV_K - 1):])
        for dst, val in zip(outs[1::2], new):
            dst.append(val)

    kp, ks, vp, vs, cp, cs, fp, fs = (jnp.stack(o) for o in outs)
    return (xp.reshape(bsz, seq, d_model), xs.reshape(n_seq, t_new, d_model), kp, vp, ks, vs, cp, cs, fp, fs)
```

```python
import functools

import jax
import jax.numpy as jnp
import numpy as np
from jax import lax
from jax.experimental import pallas as pl
from jax.experimental.pallas import tpu as pltpu

N_HEADS = 8
HEAD_DIM = 64
ATTN_WIDTH = N_HEADS * HEAD_DIM
ROT_DIM = HEAD_DIM // 4
ROPE_THETA = 500000.0
MOBA_BLOCK = 256
MOBA_TOPK = 3
CONV_K = 3
LN_EPS = 1e-5

LANES = 128
SUBLANES = 8
BF16_ROWS = 16
ROW_TILE = 512
ROW_TILE_SHORT = 256
FF_CHUNK = 256
PAGES_PER_STEP = 32
PAGE_SLOTS = 3
PAGE_LOOKAHEAD = 2
PAST_BLOCKS_PER_TRIP = 4
HEAD_TILES_PER_STEP = 4
VMEM_LIMIT = 56 * 1024 * 1024

_NT = (((1,), (1,)), ((), ()))


def _bf16(x):
    return x.astype(jnp.bfloat16)


def _dot(a, b):
    return jnp.dot(a, b, preferred_element_type=jnp.float32)


def _dot_nt(a, b):
    return lax.dot_general(a, b, _NT, preferred_element_type=jnp.float32)


def _layer_norm(x, g, b):
    mu = jnp.mean(x, axis=-1, keepdims=True)
    xc = x - mu
    var = jnp.mean(xc * xc, axis=-1, keepdims=True)
    return xc * lax.rsqrt(var + LN_EPS) * g + b


def _resident(arr):
    return pl.BlockSpec(arr.shape, lambda *_: (0,) * arr.ndim, pipeline_mode=pl.Buffered(1))


def _causal_conv3(u3, prev3, w_ref, cols):
    sub = lax.broadcasted_iota(jnp.int32, u3.shape, 1)
    s1 = jnp.where(sub < 1, pltpu.roll(prev3, 1, 1), pltpu.roll(u3, 1, 1))
    s2 = jnp.where(sub < 2, pltpu.roll(prev3, 2, 1), pltpu.roll(u3, 2, 1))
    return s2 * w_ref[0:1, cols] + s1 * w_ref[1:2, cols] + u3 * w_ref[2:3, cols]


def _prev_groups(u3, carry):
    return jnp.concatenate([carry[None], u3[:-1]], axis=0)


def _rope_lanes(z, ra_ref, rb_ref, rc_ref):
    outs = []
    for c in range(z.shape[1] // LANES):
        zc = z[:, c * LANES:(c + 1) * LANES]
        outs.append(zc * ra_ref[...]
                    + pltpu.roll(zc, LANES - ROT_DIM // 2, 1) * rb_ref[...]
                    + pltpu.roll(zc, ROT_DIM // 2, 1) * rc_ref[...])
    return jnp.concatenate(outs, axis=1)


def _rope_rows(zt, cos_ref, sin_ref):
    half = ROT_DIM // 2
    cos, sin = cos_ref[...], sin_ref[...]
    parts = []
    for h in range(N_HEADS):
        r0 = h * HEAD_DIM
        x1, x2 = zt[r0:r0 + half], zt[r0 + half:r0 + ROT_DIM]
        parts += [x1 * cos - x2 * sin, x2 * cos + x1 * sin, zt[r0 + ROT_DIM:r0 + HEAD_DIM]]
    return jnp.concatenate(parts, axis=0)


def _gated_conv(xb, w_ref, cw_ref, prev3_fn, cg_ref):
    a = ATTN_WIDTH
    c = cw_ref.shape[1]
    rows = xb.shape[0]
    gb = _dot(xb, w_ref[:, 3 * a:3 * a + c])
    u = _dot(xb, w_ref[:, 3 * a + c:3 * a + 2 * c]) * _dot(xb, w_ref[:, 3 * a + 2 * c:3 * a + 3 * c])
    u3 = u.reshape(rows // SUBLANES, SUBLANES, c)
    y = _causal_conv3(u3, prev3_fn(u3), cw_ref, slice(None)).reshape(rows, c)
    cg_ref[...] = _bf16(gb * y)
    return u3


def _proj_long_kernel(x_ref, w_ref, wt_ref, ra_ref, rb_ref, rc_ref, cos_ref, sin_ref, cw_ref, st_ref,
                      qt_ref, k_ref, kt_ref, vt_ref, vtb_ref, cg_ref, tail_ref, carry_ref):
    a = ATTN_WIDTH
    xb = _bf16(x_ref[...])
    qkvt = _dot_nt(wt_ref[...], xb)
    qt_ref[0] = _bf16(_rope_rows(qkvt[0:a], cos_ref, sin_ref) * (HEAD_DIM ** -0.5))
    kt_ref[0] = _rope_rows(qkvt[a:2 * a], cos_ref, sin_ref)
    vt = qkvt[2 * a:3 * a]
    vt_ref[0] = vt
    for c in range(vtb_ref.shape[1]):
        vtb_ref[0, c] = _bf16(vt[:, c * MOBA_BLOCK:(c + 1) * MOBA_BLOCK])
    k_ref[...] = _bf16(_rope_lanes(_dot(xb, w_ref[:, a:2 * a]), ra_ref, rb_ref, rc_ref))

    @pl.when(pl.program_id(1) == 0)
    def _():
        carry_ref[...] = st_ref[0]

    u3 = _gated_conv(xb, w_ref, cw_ref, lambda u3: _prev_groups(u3, carry_ref[...]), cg_ref)
    carry_ref[...] = u3[-1]
    tail_ref[0] = u3[-1]


def _proj_short_kernel(x_ref, w_ref, ra_ref, rb_ref, rc_ref, cw_ref, st_ref,
                       q_ref, k_ref, v_ref, cg_ref, tail_ref):
    a = ATTN_WIDTH
    xb = _bf16(x_ref[...])
    q = _rope_lanes(_dot(xb, w_ref[:, 0:a]), ra_ref, rb_ref, rc_ref)
    q_ref[...] = _bf16(q * (HEAD_DIM ** -0.5))
    k_ref[...] = _rope_lanes(_dot(xb, w_ref[:, a:2 * a]), ra_ref, rb_ref, rc_ref)
    v_ref[...] = _dot(xb, w_ref[:, 2 * a:3 * a])
    tail_ref[...] = _gated_conv(xb, w_ref, cw_ref, lambda u3: st_ref[...], cg_ref)


def _proj_long(x2d, w_in_b, w_qkvt_b, lane_tabs, row_tabs, conv_w, state8, bsz, seq):
    rows, d = x2d.shape
    tm = ROW_TILE
    nt = seq // tm
    a, c = ATTN_WIDTH, conv_w.shape[1]
    nblk = tm // MOBA_BLOCK
    row_blk = lambda b, t: (b * nt + t, 0)
    tab_spec = pl.BlockSpec((tm, LANES), lambda b, t: (t, 0))
    rtab_spec = pl.BlockSpec((ROT_DIM // 2, tm), lambda b, t: (0, t))
    st_spec = pl.BlockSpec((1, SUBLANES, c), lambda b, t: (b, 0, 0))
    kt_spec = pl.BlockSpec((1, a, tm), lambda b, t: (b, 0, t))
    vtb_spec = pl.BlockSpec((1, nblk, a, MOBA_BLOCK), lambda b, t: (b, t, 0, 0))
    f32, bf = jnp.float32, jnp.bfloat16
    out_shape = (
        jax.ShapeDtypeStruct((bsz, a, seq), bf),
        jax.ShapeDtypeStruct((rows, a), bf),
        jax.ShapeDtypeStruct((bsz, a, seq), f32), jax.ShapeDtypeStruct((bsz, a, seq), f32),
        jax.ShapeDtypeStruct((bsz, seq // MOBA_BLOCK, a, MOBA_BLOCK), bf),
        jax.ShapeDtypeStruct((rows, c), bf),
        jax.ShapeDtypeStruct(state8.shape, f32),
    )
    return pl.pallas_call(
        _proj_long_kernel,
        grid=(bsz, nt),
        in_specs=[pl.BlockSpec((tm, d), row_blk), _resident(w_in_b), _resident(w_qkvt_b),
                  tab_spec, tab_spec, tab_spec, rtab_spec, rtab_spec, _resident(conv_w), st_spec],
        out_specs=[kt_spec, pl.BlockSpec((tm, a), row_blk), kt_spec, kt_spec, vtb_spec,
                   pl.BlockSpec((tm, c), row_blk), st_spec],
        out_shape=out_shape,
        scratch_shapes=[pltpu.VMEM((SUBLANES, c), f32)],
        compiler_params=pltpu.CompilerParams(
            dimension_semantics=("arbitrary", "arbitrary"), vmem_limit_bytes=VMEM_LIMIT),
        name="proj_long",
    )(x2d, w_in_b, w_qkvt_b, *lane_tabs, *row_tabs, conv_w, state8)


def _proj_short(x2d, w_in_b, lane_tabs, conv_w, state8):
    rows, d = x2d.shape
    tm = ROW_TILE_SHORT
    a, c = ATTN_WIDTH, conv_w.shape[1]
    row_blk = lambda t: (t, 0)
    tab_spec = pl.BlockSpec((tm, LANES), lambda t: (0, 0))
    st_spec = pl.BlockSpec((tm // SUBLANES, SUBLANES, c), lambda t: (t, 0, 0))
    f32, bf = jnp.float32, jnp.bfloat16
    out_shape = (
        jax.ShapeDtypeStruct((rows, a), bf), jax.ShapeDtypeStruct((rows, a), f32),
        jax.ShapeDtypeStruct((rows, a), f32), jax.ShapeDtypeStruct((rows, c), bf),
        jax.ShapeDtypeStruct(state8.shape, f32),
    )
    return pl.pallas_call(
        _proj_short_kernel,
        grid=(rows // tm,),
        in_specs=[pl.BlockSpec((tm, d), row_blk), _resident(w_in_b),
                  tab_spec, tab_spec, tab_spec, _resident(conv_w), st_spec],
        out_specs=[pl.BlockSpec((tm, a), row_blk)] * 3 + [pl.BlockSpec((tm, c), row_blk), st_spec],
        out_shape=out_shape,
        compiler_params=pltpu.CompilerParams(
            dimension_semantics=("arbitrary",), vmem_limit_bytes=VMEM_LIMIT),
        name="proj_short",
    )(x2d, w_in_b, *lane_tabs, conv_w, state8)


def _rope_angles(pos):
    half = ROT_DIM // 2
    inv = 1.0 / (ROPE_THETA ** (jnp.arange(half, dtype=jnp.float32) / half))
    ang = pos.astype(jnp.float32)[:, None] * inv[None, :]
    return jnp.cos(ang), jnp.sin(ang)


def _rope_lane_tables(cos, sin):
    t, half = cos.shape
    rest = HEAD_DIM - ROT_DIM
    ones, zeros = jnp.ones((t, rest), jnp.float32), jnp.zeros((t, rest), jnp.float32)
    zh = jnp.zeros((t, half), jnp.float32)
    ta = jnp.concatenate([cos, cos, ones], axis=1)
    tb = jnp.concatenate([-sin, zh, zeros], axis=1)
    tc = jnp.concatenate([zh, sin, zeros], axis=1)
    return tuple(jnp.tile(x, (1, LANES // HEAD_DIM)) for x in (ta, tb, tc))


def _top_mask(gate, n_valid, axis):
    idx = lax.broadcasted_iota(jnp.int32, gate.shape, axis)
    n = gate.shape[axis]
    g = jnp.where(idx < n_valid, gate, -jnp.inf)
    sel = jnp.zeros(gate.shape, jnp.bool_)
    for _ in range(MOBA_TOPK):
        mx = jnp.max(g, axis=axis, keepdims=True)
        first = jnp.min(jnp.where(g == mx, idx, n), axis=axis, keepdims=True)
        pick = (idx == first) & (mx > -jnp.inf)
        sel = sel | pick
        g = jnp.where(pick, -jnp.inf, g)
    return sel


def _moba_prompt_kernel(qt_ref, k_ref, vt_ref, o_ref, km_ref, bias_ref, m_ref, acc_ref):
    i = pl.program_id(2)
    blk = MOBA_BLOCK
    nb = vt_ref.shape[1]
    tiles = km_ref.shape[0]
    nbp = -(-nb // BF16_ROWS) * BF16_ROWS
    f32, bf = jnp.float32, jnp.bfloat16

    @pl.when(i == 0)
    def _():
        for ti in range(tiles):
            kf = k_ref[0, :, ti * LANES:(ti + 1) * LANES].astype(f32).reshape(nb, blk, LANES)
            km_ref[ti] = jnp.zeros((LANES, LANES), f32)
            km_ref[ti, 0:nb, :] = jnp.sum(kf, axis=1) * (1.0 / blk)

    row_q = lax.broadcasted_iota(jnp.int32, (LANES, blk), 0)
    head0 = _bf16(jnp.where(row_q < HEAD_DIM, 1.0, 0.0))
    head1 = _bf16(jnp.where(row_q < HEAD_DIM, 0.0, 1.0))
    ones_rows = jnp.ones((BF16_ROWS, blk), bf)

    def make_block_fn(ti):
        lanes = slice(ti * LANES, (ti + 1) * LANES)
        qt = qt_ref[0, lanes, :]
        q2 = jnp.concatenate([qt * head0, qt * head1], axis=1)
        gate = _dot(km_ref[ti], q2.astype(f32))[0:nbp]
        row_g = lax.broadcasted_iota(jnp.int32, gate.shape, 0)
        visible = _top_mask(gate, i, 0) | (row_g == i)
        bias_ref[ti] = jnp.where(visible, 0.0, -jnp.inf)

        def block(j):
            start = pl.multiple_of(j * blk, blk)
            st = _dot(k_ref[0, pl.ds(start, blk), lanes], q2) + bias_ref[ti, pl.ds(j, 1), :]
            vv = jnp.concatenate([vt_ref[0, j, lanes, :], ones_rows], axis=0)
            return st, vv

        return block

    block_fns = [make_block_fn(ti) for ti in range(tiles)]

    def fold(state, sts, vvs):
        m_new = functools.reduce(jnp.maximum, [jnp.max(st, axis=0, keepdims=True) for st in sts])
        if state is not None:
            m_new = jnp.maximum(state[0], m_new)
        pv = functools.reduce(jnp.add, [_dot(vv, _bf16(jnp.exp(st - m_new))) for st, vv in zip(sts, vvs)])
        if state is not None:
            pv = state[1] * jnp.exp(state[0] - m_new) + pv
        return m_new, pv

    per_trip = PAST_BLOCKS_PER_TRIP
    n_tail = i % per_trip
    for k in range(per_trip):
        @pl.when(n_tail == k)
        def _(k=k):
            for ti, block in enumerate(block_fns):
                st, vv = block(i)
                key = lax.broadcasted_iota(jnp.int32, st.shape, 0)
                col = lax.broadcasted_iota(jnp.int32, st.shape, 1)
                qpos = jnp.where(col >= blk, col - blk, col)
                sts, vvs = [jnp.where(key <= qpos, st, -jnp.inf)], [vv]
                for u in range(k):
                    st, vv = block(i - k + u)
                    sts.append(st)
                    vvs.append(vv)
                m_ref[ti], acc_ref[ti] = fold(None, sts, vvs)

    def body(t, carry):
        for ti, block in enumerate(block_fns):
            sts, vvs = zip(*[block(per_trip * t + u) for u in range(per_trip)])
            m_ref[ti], acc_ref[ti] = fold((m_ref[ti], acc_ref[ti]), sts, vvs)
        return carry

    lax.fori_loop(0, i // per_trip, body, 0)

    for ti in range(tiles):
        acc = acc_ref[ti]
        ot = acc[0:LANES] / acc[LANES:LANES + 1]
        x = jnp.concatenate([ot[0:HEAD_DIM, 0:blk], ot[HEAD_DIM:LANES, blk:2 * blk]], axis=0)
        o_ref[0, :, ti * LANES:(ti + 1) * LANES] = _bf16(x.T)


def _moba_prompt(qt, k_rm, vtb, bsz, seq):
    blk = MOBA_BLOCK
    nb = seq // blk
    tiles = HEAD_TILES_PER_STEP
    width = tiles * LANES
    assert nb <= LANES and ATTN_WIDTH % width == 0
    k3 = k_rm.reshape(bsz, seq, ATTN_WIDTH)
    qt_spec = pl.BlockSpec((1, width, blk), lambda b, h, i: (b, h, i))
    k_spec = pl.BlockSpec((1, seq, width), lambda b, h, i: (b, 0, h))
    vt_spec = pl.BlockSpec((1, nb, width, blk), lambda b, h, i: (b, 0, h, 0))
    out = pl.pallas_call(
        _moba_prompt_kernel,
        grid=(bsz, ATTN_WIDTH // width, nb),
        in_specs=[qt_spec, k_spec, vt_spec],
        out_specs=pl.BlockSpec((1, blk, width), lambda b, h, i: (b, i, h)),
        out_shape=jax.ShapeDtypeStruct((bsz, seq, ATTN_WIDTH), jnp.bfloat16),
        scratch_shapes=[pltpu.VMEM((tiles, LANES, LANES), jnp.float32),
                        pltpu.VMEM((tiles, -(-nb // BF16_ROWS) * BF16_ROWS, 2 * blk), jnp.float32),
                        pltpu.VMEM((tiles, 1, 2 * blk), jnp.float32),
                        pltpu.VMEM((tiles, LANES + BF16_ROWS, 2 * blk), jnp.float32)],
        compiler_params=pltpu.CompilerParams(
            dimension_semantics=("arbitrary", "arbitrary", "arbitrary"), vmem_limit_bytes=VMEM_LIMIT),
        name="moba_prompt",
    )(qt, k3, vtb)
    return out.reshape(bsz * seq, ATTN_WIDTH)


def _moba_sample_kernel(pt_ref, q_ref, k1_ref, v1_ref, ck_ref, cv_ref, o_ref,
                        buf_ref, sem_ref, wq_ref, s_ref, bias_ref, gate_ref, bmax_ref, m_ref, l_ref, acc_ref):
    seq, g = pl.program_id(0), pl.program_id(1)
    steps = pl.num_programs(1)
    ks = s_ref.shape[0]
    n_slot, n_pg = buf_ref.shape[0], buf_ref.shape[1]
    step = seq * steps + g
    total_steps = pl.num_programs(0) * steps

    def page_copy(src_ref, page_id, slot, p):
        return pltpu.make_async_copy(src_ref.at[page_id], buf_ref.at[slot, p], sem_ref.at[slot])

    def fetch(ahead):
        g_raw = g + ahead
        wrap = (g_raw >= steps).astype(jnp.int32)
        seq_f, g_f = seq + wrap, g_raw - wrap * steps
        slot = (step + ahead) % n_slot

        @pl.when(g_f < ks)
        def _():
            for p in range(n_pg):
                page_copy(ck_ref, pt_ref[seq_f, g_f * n_pg + p], slot, p).start()

        @pl.when(g_f >= ks)
        def _():
            for p in range(n_pg):
                page_copy(cv_ref, pt_ref[seq_f, (g_f - ks) * n_pg + p], slot, p).start()

    @pl.when(step == 0)
    def _():
        for ahead in range(PAGE_LOOKAHEAD):
            fetch(ahead)

    @pl.when(step + PAGE_LOOKAHEAD < total_steps)
    def _():
        fetch(PAGE_LOOKAHEAD)

    slot = step % n_slot
    for p in range(n_pg):
        page_copy(ck_ref, 0, slot, p).wait()
    k_pages = v_pages = [buf_ref.at[slot, p] for p in range(n_pg)]

    f32 = jnp.float32
    t_new = q_ref.shape[1]
    rows = N_HEADS * t_new
    page = buf_ref.shape[3]
    blk = MOBA_BLOCK
    pages_per_blk = blk // page
    blks_per_step = n_pg // pages_per_blk
    nblk = ks * blks_per_step
    row = lax.broadcasted_iota(jnp.int32, (rows, ATTN_WIDTH), 0)
    lane = lax.broadcasted_iota(jnp.int32, (rows, ATTN_WIDTH), 1)
    own_head = (row // t_new) == (lane // HEAD_DIM)

    @pl.when(g == 0)
    def _():
        q8 = q_ref[0].astype(f32)
        wq_ref[...] = _bf16(jnp.where(own_head, jnp.concatenate([q8] * N_HEADS, axis=0), 0.0))
        gate_ref[...] = jnp.zeros(gate_ref.shape, f32)
        bmax_ref[...] = jnp.zeros(bmax_ref.shape, f32)

    @pl.when(g < ks)
    def _():
        wq = wq_ref[...]
        lane_g = lax.broadcasted_iota(jnp.int32, (rows, LANES), 1)
        gate, bmax = gate_ref[...], bmax_ref[...]
        for b in range(blks_per_step):
            pg = range(b * pages_per_blk, (b + 1) * pages_per_blk)
            s_blk = jnp.concatenate([_dot(wq, _bf16(k_pages[p][...])) for p in pg], axis=1)
            s_ref[g, :, b * blk:(b + 1) * blk] = s_blk
            hit = lane_g == g * blks_per_step + b
            gate = jnp.where(hit, jnp.sum(s_blk, axis=-1, keepdims=True), gate)
            bmax = jnp.where(hit, jnp.max(s_blk, axis=-1, keepdims=True), bmax)
        gate_ref[...] = gate
        bmax_ref[...] = bmax

    @pl.when(g == ks - 1)
    def _():
        wq = wq_ref[...]
        sel = _top_mask(gate_ref[...], nblk, 1)
        bias = jnp.where(sel, 0.0, -jnp.inf)
        for gi in range(ks):
            shift = (LANES - gi * blks_per_step) % LANES
            bias_ref[gi] = pltpu.roll(bias, shift, 1) if shift else bias
        zpad = jnp.zeros((LANES - t_new, ATTN_WIDTH), jnp.bfloat16)
        k_own = jnp.concatenate([_bf16(k1_ref[0]), zpad], axis=0)
        v_own = jnp.concatenate([_bf16(v1_ref[0]), zpad], axis=0)
        s_own = _dot_nt(wq, k_own)
        r_o = lax.broadcasted_iota(jnp.int32, s_own.shape, 0)
        c_o = lax.broadcasted_iota(jnp.int32, s_own.shape, 1)
        s_own = jnp.where(c_o <= r_o % t_new, s_own, -jnp.inf)
        m = jnp.maximum(jnp.max(jnp.where(sel, bmax_ref[...], -jnp.inf), axis=-1, keepdims=True),
                        jnp.max(s_own, axis=-1, keepdims=True))
        m_ref[...] = m
        p_own = jnp.exp(s_own - m)
        l_ref[...] = jnp.sum(p_own, axis=-1, keepdims=True)
        acc_ref[...] = _dot(_bf16(p_own), v_own)

    @pl.when(g >= ks)
    def _():
        gv = g - ks
        m, bias = m_ref[...], bias_ref[gv]
        den = jnp.zeros((rows, blk), f32)
        o = acc_ref[...]
        for b in range(blks_per_step):
            p_blk = jnp.exp(s_ref[gv, :, b * blk:(b + 1) * blk] + bias[:, b:b + 1] - m)
            den = den + p_blk
            pb = _bf16(p_blk)
            for x in range(pages_per_blk):
                o = o + _dot_nt(pb[:, x * page:(x + 1) * page], _bf16(v_pages[b * pages_per_blk + x][...]))
        acc_ref[...] = o
        l_ref[...] = l_ref[...] + jnp.sum(den, axis=-1, keepdims=True)

    @pl.when(g == 2 * ks - 1)
    def _():
        o64 = jnp.where(own_head, acc_ref[...] / l_ref[...], 0.0)
        o8 = o64[0:t_new]
        for h in range(1, N_HEADS):
            o8 = o8 + o64[h * t_new:(h + 1) * t_new]
        o_ref[0] = _bf16(o8)


def _moba_sample(qb, k_new, v_new, cache_k, cache_v, page_table, n_seq, t_new):
    n_phys, page = cache_k.shape[0], cache_k.shape[1]
    ck = cache_k.transpose(0, 2, 3, 1).reshape(n_phys, ATTN_WIDTH, page)
    cv = cache_v.transpose(0, 2, 3, 1).reshape(n_phys, ATTN_WIDTH, page)
    n_pages = page_table.shape[1]
    pps = PAGES_PER_STEP
    ks = n_pages // pps
    rows = N_HEADS * t_new
    assert n_pages * page // MOBA_BLOCK <= LANES
    assert PAGE_LOOKAHEAD < PAGE_SLOTS and PAGE_LOOKAHEAD <= 2 * ks
    seq_spec = pl.BlockSpec((1, t_new, ATTN_WIDTH), lambda s, g, pt: (s, 0, 0))
    hbm_spec = pl.BlockSpec(memory_space=pl.ANY)

    f32, bf = jnp.float32, jnp.bfloat16
    out = pl.pallas_call(
        _moba_sample_kernel,
        grid_spec=pltpu.PrefetchScalarGridSpec(
            num_scalar_prefetch=1,
            grid=(n_seq, 2 * ks),
            in_specs=[seq_spec, seq_spec, seq_spec, hbm_spec, hbm_spec],
            out_specs=seq_spec,
            scratch_shapes=[pltpu.VMEM((PAGE_SLOTS, pps, ATTN_WIDTH, page), f32),
                            pltpu.SemaphoreType.DMA((PAGE_SLOTS,)),
                            pltpu.VMEM((rows, ATTN_WIDTH), bf),
                            pltpu.VMEM((ks, rows, pps * page), f32),
                            pltpu.VMEM((ks, rows, LANES), f32),
                            pltpu.VMEM((rows, LANES), f32),
                            pltpu.VMEM((rows, LANES), f32),
                            pltpu.VMEM((rows, 1), f32),
                            pltpu.VMEM((rows, 1), f32),
                            pltpu.VMEM((rows, ATTN_WIDTH), f32)]),
        out_shape=jax.ShapeDtypeStruct((n_seq, t_new, ATTN_WIDTH), bf),
        compiler_params=pltpu.CompilerParams(
            dimension_semantics=("arbitrary", "arbitrary"), vmem_limit_bytes=VMEM_LIMIT),
        name="moba_sample",
    )(page_table, qb.reshape(n_seq, t_new, ATTN_WIDTH), k_new.reshape(n_seq, t_new, ATTN_WIDTH),
      v_new.reshape(n_seq, t_new, ATTN_WIDTH), ck, cv)
    return out.reshape(n_seq * t_new, ATTN_WIDTH)


def _post_kernel(x_ref, at_ref, cg_ref, st_ref, wo_ref, g1_ref, b1_ref, wup_ref, fw_ref, fb_ref, wdn_ref,
                 g2_ref, b2_ref, y_ref, tail_ref, carry_ref, hid_ref, *, long_seq, alpha):
    rows = x_ref.shape[0]
    d_ff = wdn_ref.shape[0]
    mix = _dot(jnp.concatenate([at_ref[...], cg_ref[...]], axis=1), wo_ref[...])
    x1 = _layer_norm(alpha * x_ref[...] + mix, g1_ref[...], b1_ref[...])
    x1b = _bf16(x1)

    if long_seq:
        @pl.when(pl.program_id(1) == 0)
        def _():
            carry_ref[...] = st_ref[0]

    for c0 in range(0, d_ff, FF_CHUNK):
        cols = slice(c0, c0 + FF_CHUNK)
        up_a = _dot(x1b, wup_ref[:, cols])
        up_g = _dot(x1b, wup_ref[:, d_ff + c0:d_ff + c0 + FF_CHUNK])
        a3 = up_a.reshape(rows // SUBLANES, SUBLANES, FF_CHUNK)
        if long_seq:
            prev3 = _prev_groups(a3, carry_ref[:, cols])
            carry_ref[:, cols] = a3[-1]
            tail_ref[0, :, cols] = a3[-1]
        else:
            prev3 = st_ref[:, :, cols]
            tail_ref[:, :, cols] = a3
        a_c = (_causal_conv3(a3, prev3, fw_ref, cols) + fb_ref[0:1, cols]).reshape(rows, FF_CHUNK)
        act = 0.5 * a_c * (1.0 + lax.erf(a_c * np.float32(np.sqrt(0.5))))
        hid_ref[:, cols] = _bf16(act * up_g)
    f = _dot(hid_ref[...], wdn_ref[...])
    y_ref[...] = _layer_norm(alpha * x1 + f, g2_ref[...], b2_ref[...])


def _post(x2d, attn, cg, state8, w_o_b, ln1_g, ln1_b, w_up_b, fcw, fcb, w_dn_b, ln2_g, ln2_b,
          *, nb, nt, long_seq, alpha):
    rows, d = x2d.shape
    tm = rows // (nb * nt)
    d_ff = w_dn_b.shape[0]
    row_blk = lambda b, t: (b * nt + t, 0)
    if long_seq:
        st_spec = pl.BlockSpec((1, SUBLANES, d_ff), lambda b, t: (b, 0, 0))
    else:
        st_spec = pl.BlockSpec((tm // SUBLANES, SUBLANES, d_ff), lambda b, t: (t, 0, 0))
    vec = lambda v: v.reshape(1, -1)
    ln1_g, ln1_b, fcb, ln2_g, ln2_b = (vec(v) for v in (ln1_g, ln1_b, fcb, ln2_g, ln2_b))
    consts = (w_o_b, ln1_g, ln1_b, w_up_b, fcw, fcb, w_dn_b, ln2_g, ln2_b)
    return pl.pallas_call(
        functools.partial(_post_kernel, long_seq=long_seq, alpha=alpha),
        grid=(nb, nt),
        in_specs=[pl.BlockSpec((tm, d), row_blk),
                  pl.BlockSpec((tm, attn.shape[1]), row_blk),
                  pl.BlockSpec((tm, cg.shape[1]), row_blk),
                  st_spec] + [_resident(w) for w in consts],
        out_specs=[pl.BlockSpec((tm, d), row_blk), st_spec],
        out_shape=(jax.ShapeDtypeStruct((rows, d), jnp.float32),
                   jax.ShapeDtypeStruct(state8.shape, jnp.float32)),
        scratch_shapes=[pltpu.VMEM((SUBLANES, d_ff), jnp.float32),
                        pltpu.VMEM((tm, d_ff), jnp.bfloat16)],
        compiler_params=pltpu.CompilerParams(
            dimension_semantics=("arbitrary", "arbitrary"), vmem_limit_bytes=VMEM_LIMIT),
        name="post_long" if long_seq else "post_short",
    )(x2d, attn, cg, state8, *consts)


def _pad_state(state):
    return jnp.pad(state, ((0, 0), (SUBLANES - (CONV_K - 1), 0), (0, 0)))


def kernel(x_prompt, x_sample, cache_k, cache_v, page_table, state_conv, state_ffn_conv, w_in, conv_w, w_o,
           ln1_g, ln1_b, w_up, ffn_conv_w, ffn_conv_b, w_down, ln2_g, ln2_b):
    depth = w_in.shape[0]
    bsz, seq, d_model = x_prompt.shape
    n_seq, t_new, _ = x_sample.shape
    conv_width = conv_w.shape[2]
    d_ff = w_down.shape[1]
    past_len = page_table.shape[1] * cache_k.shape[2]
    alpha = (2.0 * depth) ** 0.25
    a = ATTN_WIDTH
    assert t_new == SUBLANES and seq % ROW_TILE == 0 and (n_seq * t_new) % ROW_TILE_SHORT == 0
    assert past_len % MOBA_BLOCK == 0 and past_len // MOBA_BLOCK >= MOBA_TOPK and d_ff % FF_CHUNK == 0
    assert page_table.shape[1] % PAGES_PER_STEP == 0 and (PAGES_PER_STEP * cache_k.shape[2]) % MOBA_BLOCK == 0

    cos_p, sin_p = _rope_angles(jnp.arange(seq, dtype=jnp.int32))
    cos_s, sin_s = _rope_angles(past_len + jnp.arange(t_new, dtype=jnp.int32))
    lane_tabs_p = _rope_lane_tables(cos_p, sin_p)
    row_tabs_p = (cos_p.T, sin_p.T)
    lane_tabs_s = tuple(jnp.tile(t, (ROW_TILE_SHORT // t_new, 1)) for t in _rope_lane_tables(cos_s, sin_s))
    grid_p = dict(nb=bsz, nt=seq // ROW_TILE, long_seq=True)
    grid_s = dict(nb=1, nt=n_seq * t_new // ROW_TILE_SHORT, long_seq=False)
    zeros_conv = jnp.zeros((bsz, SUBLANES, conv_width), jnp.float32)
    zeros_ffn = jnp.zeros((bsz, SUBLANES, d_ff), jnp.float32)

    xp = x_prompt.reshape(bsz * seq, d_model)
    xs = x_sample.reshape(n_seq * t_new, d_model)
    outs = [[] for _ in range(8)]
    for l in range(depth):
        w_in_b, w_o_b, w_up_b, w_dn_b = _bf16(w_in[l]), _bf16(w_o[l]), _bf16(w_up[l]), _bf16(w_down[l])
        w_qkvt_b = w_in_b[:, 0:3 * a].T
        post_w = (w_o_b, ln1_g[l], ln1_b[l], w_up_b, ffn_conv_w[l], ffn_conv_b[l], w_dn_b, ln2_g[l], ln2_b[l])

        qt, k_rm, kt, vt, vtb, cg, tail = _proj_long(
            xp, w_in_b, w_qkvt_b, lane_tabs_p, row_tabs_p, conv_w[l], zeros_conv, bsz, seq)
        attn = _moba_prompt(qt, k_rm, vtb, bsz, seq)
        xp, ffn_tail = _post(xp, attn, cg, zeros_ffn, *post_w, alpha=alpha, **grid_p)
        new = (kt.reshape(bsz, N_HEADS, HEAD_DIM, seq).transpose(0, 3, 1, 2),
               vt.reshape(bsz, N_HEADS, HEAD_DIM, seq).transpose(0, 3, 1, 2),
               tail[:, -(CONV_K - 1):], ffn_tail[:, -(CONV_K - 1):])
        for dst, val in zip(outs[0::2], new):
            dst.append(val)

        qb, k, v, cg, tail = _proj_short(xs, w_in_b, lane_tabs_s, conv_w[l], _pad_state(state_conv[l]))
        attn = _moba_sample(qb, k, v, cache_k[l], cache_v[l], page_table, n_seq, t_new)
        xs, ffn_tail = _post(xs, attn, cg, _pad_state(state_ffn_conv[l]), *post_w, alpha=alpha, **grid_s)
        new = (k.reshape(n_seq, t_new, N_HEADS, HEAD_DIM), v.reshape(n_seq, t_new, N_HEADS, HEAD_DIM),
               tail[:, -(CONV_K - 1):], ffn_tail[:, -(CONV_K - 1):])
        for dst, val in zip(outs[1::2], new):
            dst.append(val)

    kp, ks, vp, vs, cp, cs, fp, fs = (jnp.stack(o) for o in outs)
    return (xp.reshape(bsz, seq, d_model), xs.reshape(n_seq, t_new, d_model), kp, vp, ks, vs, cp, cs, fp, fs)
```
